```python
import jax, jax.numpy as jnp
from jax import lax
import numpy as np

D_MODEL = 1024
BATCH = 16
SEQ = 256
DEPTH = 1
DEC_BATCH = 2
DEC_SEQ = 1024
PAST_LEN = 512

GRID_W = 64
GLA_HEADS = 4
GLA_DK = 64
GLA_DV = 128
GLA_QK = GLA_HEADS * GLA_DK
GLA_V = GLA_HEADS * GLA_DV
GLA_LOWRANK = 16
GLA_GATE_NORM = 16.0
GLA_CHUNK = 64
NA_HEADS = 8
NA_HD = 64
NA_W = NA_HEADS * NA_HD
NA_WIN_ROWS = 8
NA_WIN_COLS = 16
D_FF = 2816
CONV_W = 3
EPS = 1e-6
Q_BLOCK = 128
IN_SIZES = (GLA_QK, GLA_QK, GLA_V, GLA_V, GLA_LOWRANK, GLA_LOWRANK, NA_W, NA_W, NA_W, D_MODEL, D_MODEL)
D_IN = sum(IN_SIZES)

kernel_name = 'hybrid_gla_natten_prefix_dit_step'


def rmsnorm(x, g):
    xf = x.astype(jnp.float32)
    y = xf * lax.rsqrt(jnp.mean(xf * xf, axis=-1, keepdims=True) + EPS)
    return (y * g.astype(jnp.float32)).astype(x.dtype)


def ada_mod(cond, w, b):
    m = (jax.nn.silu(cond) @ w + b)[:, None, :]
    return jnp.split(m, 6, axis=-1)


def split_in(u):
    idx = np.cumsum(IN_SIZES)[:-1].tolist()
    return jnp.split(u, idx, axis=-1)


def gla_scan(q, k, v, log_a, s0):
    B, T, H, DK = q.shape
    DV = v.shape[-1]
    C = GLA_CHUNK
    N = T // C
    f32 = jnp.float32
    qc = q.astype(f32).reshape(B, N, C, H, DK)
    kc = k.astype(f32).reshape(B, N, C, H, DK)
    vc = v.astype(f32).reshape(B, N, C, H, DV)
    bc = jnp.cumsum(log_a.astype(f32).reshape(B, N, C, H, DK), axis=2)
    b_last = bc[:, :, -1]
    causal = jnp.tril(jnp.ones((C, C), dtype=bool))[None, None, :, :, None, None]
    diff = bc[:, :, :, None] - bc[:, :, None, :]
    decay = jnp.exp(jnp.where(causal, diff, -jnp.inf))
    scores = jnp.einsum('bnthd,bnshd,bntshd->bnhts', qc, kc, decay)
    o_intra = jnp.einsum('bnhts,bnshv->bnthv', scores, vc)
    kv_chunk = jnp.einsum('bnshd,bnshv->bnhdv', kc * jnp.exp(b_last[:, :, None] - bc), vc)

    def step(s, inp):
        a_last, kv = inp
        return jnp.exp(a_last)[..., None] * s + kv, s

    s_final, s_start = lax.scan(step, s0.astype(f32),
                                (jnp.moveaxis(b_last, 1, 0), jnp.moveaxis(kv_chunk, 1, 0)))
    o_inter = jnp.einsum('bnthd,nbhdv->bnthv', qc * jnp.exp(bc), s_start)
    o = (o_intra + o_inter).reshape(B, T, H, DV)
    return o.astype(v.dtype), s_final.astype(s0.dtype)


def gla_bidir(q, k, v, la_f, la_b, s0_f, s0_b):
    o_f, s_f = gla_scan(q, k, v, la_f, s0_f)
    flip = lambda a: jnp.flip(a, axis=1)
    o_b, s_b = gla_scan(flip(q), flip(k), flip(v), flip(la_b), s0_b)
    return o_f + flip(o_b), s_f, s_b


def ctx_attention(q, k, v):
    B, L, H, HD = q.shape
    nb = L // Q_BLOCK
    qb = jnp.moveaxis(q.reshape(B, nb, Q_BLOCK, H, HD), 1, 0)

    def one(qblk):
        s = jnp.einsum('bqhd,bkhd->bhqk', qblk, k).astype(jnp.float32) * (HD ** -0.5)
        p = jax.nn.softmax(s, axis=-1)
        return jnp.einsum('bhqk,bkhd->bqhd', p.astype(v.dtype), v)

    o = lax.map(one, qb)
    return jnp.moveaxis(o, 0, 1).reshape(B, L, H, HD)


def na_latent_attention(q, k, v, k_ctx, v_ctx, rpb):
    B, T, H, HD = q.shape
    rows = T // GRID_W
    wr = min(NA_WIN_ROWS, rows)
    wc = NA_WIN_COLS
    scale = HD ** -0.5
    qg = q.reshape(B, rows, GRID_W, H, HD)
    kg = k.reshape(B, rows, GRID_W, H, HD)
    vg = v.reshape(B, rows, GRID_W, H, HD)
    r = jnp.arange(rows)
    row_start = jnp.clip(r - wr // 2, 0, rows - wr)
    key_rows = row_start[:, None] + jnp.arange(wr)[None, :]
    k_rows = kg[:, key_rows]
    v_rows = vg[:, key_rows]
    s_loc = jnp.einsum('brqhd,brikhd->bhrqik', qg, k_rows).astype(jnp.float32) * scale
    cidx = jnp.arange(GRID_W)
    col_start = jnp.clip(cidx - wc // 2, 0, GRID_W - wc)
    in_win = (cidx[None, :] >= col_start[:, None]) & (cidx[None, :] < col_start[:, None] + wc)
    dr_idx = key_rows - r[:, None] + NA_WIN_ROWS - 1
    dc_idx = jnp.clip(cidx[None, :] - cidx[:, None], -(wc - 1), wc - 1) + NA_WIN_COLS - 1
    bias = rpb.astype(jnp.float32)[:, dr_idx[:, None, :, None], dc_idx[None, :, None, :]]
    s_loc = jnp.where(in_win[None, None, None, :, None, :], s_loc + bias[None], -jnp.inf)
    n_loc = wr * GRID_W
    s_ctx = jnp.einsum('brqhd,blhd->bhrql', qg, k_ctx).astype(jnp.float32) * scale
    s = jnp.concatenate([s_loc.reshape(B, H, rows, GRID_W, n_loc), s_ctx], axis=-1)
    p = jax.nn.softmax(s, axis=-1).astype(v.dtype)
    p_loc = p[..., :n_loc].reshape(B, H, rows, GRID_W, wr, GRID_W)
    p_ctx = p[..., n_loc:]
    o = (jnp.einsum('bhrqik,brikhd->brqhd', p_loc, v_rows)
         + jnp.einsum('bhrql,blhd->brqhd', p_ctx, v_ctx))
    return o.reshape(B, T, H, HD)


def conv_ffn(h, w_up, w_conv, b_conv, w_down):
    u = h @ w_up
    ch = u.shape[-1]
    u = lax.conv_general_dilated(u, w_conv[:, None, :], window_strides=(1,),
                                 padding=((CONV_W // 2, CONV_W // 2),),
                                 dimension_numbers=('NWC', 'WIO', 'NWC'),
                                 feature_group_count=ch) + b_conv
    a, g = jnp.split(u, 2, axis=-1)
    return (a * jax.nn.silu(g)) @ w_down


def block(x, cond, lp, ctx=None):
    (w_ada, b_ada, g_mix, w_in, w_af, b_af, w_ab, b_ab, g_gla, g_qn, g_kn, rpb,
     w_gla_o, w_na_o, w_out, g_ffn, w_up, w_conv, b_conv, w_down) = lp
    B, T, _ = x.shape
    sh1, sc1, gt1, sh2, sc2, gt2 = ada_mod(cond, w_ada, b_ada)
    h = rmsnorm(x, g_mix) * (1 + sc1) + sh1
    gq, gk, gv, gr, zf, zb, nq, nk, nv, m_gla, m_na = split_in(h @ w_in)
    q = gq.reshape(B, T, GLA_HEADS, GLA_DK) * (GLA_DK ** -0.5)
    k = gk.reshape(B, T, GLA_HEADS, GLA_DK)
    v = gv.reshape(B, T, GLA_HEADS, GLA_DV)
    la_f = (jax.nn.log_sigmoid((zf @ w_af + b_af).astype(jnp.float32)) / GLA_GATE_NORM).reshape(B, T, GLA_HEADS, GLA_DK)
    la_b = (jax.nn.log_sigmoid((zb @ w_ab + b_ab).astype(jnp.float32)) / GLA_GATE_NORM).reshape(B, T, GLA_HEADS, GLA_DK)
    if ctx is None:
        s0_f = jnp.zeros((B, GLA_HEADS, GLA_DK, GLA_DV), x.dtype)
        s0_b = s0_f
    else:
        k_ctx, v_ctx, s0_f, s0_b = ctx
    o_gla, s_f, s_b = gla_bidir(q, k, v, la_f, la_b, s0_f, s0_b)
    b_gla = (rmsnorm(o_gla, g_gla).reshape(B, T, GLA_V) * jax.nn.silu(gr)) @ w_gla_o
    qn = rmsnorm(nq.reshape(B, T, NA_HEADS, NA_HD), g_qn)
    kn = rmsnorm(nk.reshape(B, T, NA_HEADS, NA_HD), g_kn)
    vn = nv.reshape(B, T, NA_HEADS, NA_HD)
    if ctx is None:
        o_na = ctx_attention(qn, kn, vn)
    else:
        o_na = na_latent_attention(qn, kn, vn, k_ctx, v_ctx, rpb)
    b_na = o_na.reshape(B, T, NA_W) @ w_na_o
    mix = (jax.nn.sigmoid(m_gla) * b_gla + jax.nn.sigmoid(m_na) * b_na) @ w_out
    x = x + gt1 * mix
    h2 = rmsnorm(x, g_ffn) * (1 + sc2) + sh2
    x = x + gt2 * conv_ffn(h2, w_up, w_conv, b_conv, w_down)
    return x, (kn, vn, s_f, s_b)


def setup_inputs(seed: int = 0) -> dict:
    key = jax.random.key(seed)
    ks = jax.random.split(key, 28)
    nrm = lambda kk, shape, s=1.0: jax.random.normal(kk, shape, jnp.float32) * s
    gain = lambda kk, shape: 1.0 + 0.02 * jax.random.normal(kk, shape, jnp.float32)
    L = DEPTH
    return {
        'x_prompt': nrm(ks[0], (BATCH, SEQ, D_MODEL)),
        'x_sample': nrm(ks[1], (DEC_BATCH, DEC_SEQ, D_MODEL)),
        'c': nrm(ks[2], (DEC_BATCH, D_MODEL)),
        'cache_na_k': nrm(ks[3], (DEC_BATCH, L, PAST_LEN, NA_HEADS, NA_HD)),
        'cache_na_v': nrm(ks[4], (DEC_BATCH, L, PAST_LEN, NA_HEADS, NA_HD)),
        'state_gla_fwd': nrm(ks[5], (DEC_BATCH, L, GLA_HEADS, GLA_DK, GLA_DV), 2.0),
        'state_gla_bwd': nrm(ks[6], (DEC_BATCH, L, GLA_HEADS, GLA_DK, GLA_DV), 2.0),
        'c_ctx': nrm(ks[7], (D_MODEL,)),
        'w_ada': nrm(ks[8], (L, D_MODEL, 6 * D_MODEL), 0.3 * D_MODEL ** -0.5),
        'b_ada': nrm(ks[9], (L, 6 * D_MODEL), 0.02),
        'g_norm_mix': gain(ks[10], (L, D_MODEL)),
        'w_in': nrm(ks[11], (L, D_MODEL, D_IN), D_MODEL ** -0.5),
        'w_alpha_fwd': nrm(ks[12], (L, GLA_LOWRANK, GLA_QK), GLA_LOWRANK ** -0.5),
        'b_alpha_fwd': nrm(ks[13], (L, GLA_QK), 0.1),
        'w_alpha_bwd': nrm(ks[14], (L, GLA_LOWRANK, GLA_QK), GLA_LOWRANK ** -0.5),
        'b_alpha_bwd': nrm(ks[15], (L, GLA_QK), 0.1),
        'g_gla_norm': gain(ks[16], (L, GLA_DV)),
        'g_q_norm': gain(ks[17], (L, NA_HD)),
        'g_k_norm': gain(ks[18], (L, NA_HD)),
        'rpb': nrm(ks[19], (L, NA_HEADS, 2 * NA_WIN_ROWS - 1, 2 * NA_WIN_COLS - 1), 0.1),
        'w_gla_o': nrm(ks[20], (L, GLA_V, D_MODEL), GLA_V ** -0.5),
        'w_na_o': nrm(ks[21], (L, NA_W, D_MODEL), NA_W ** -0.5),
        'w_out': nrm(ks[22], (L, D_MODEL, D_MODEL), D_MODEL ** -0.5),
        'g_norm_ffn': gain(ks[23], (L, D_MODEL)),
        'w_up': nrm(ks[24], (L, D_MODEL, 2 * D_FF), D_MODEL ** -0.5),
        'w_conv': nrm(ks[25], (L, CONV_W, 2 * D_FF), CONV_W ** -0.5),
        'b_conv': nrm(ks[26], (L, 2 * D_FF), 0.02),
        'w_down': nrm(ks[27], (L, D_FF, D_MODEL), D_FF ** -0.5),
    }


def reference(x_prompt, x_sample, c, cache_na_k, cache_na_v, state_gla_fwd, state_gla_bwd,
              c_ctx, w_ada, b_ada, g_norm_mix, w_in, w_alpha_fwd, b_alpha_fwd, w_alpha_bwd,
              b_alpha_bwd, g_gla_norm, g_q_norm, g_k_norm, rpb, w_gla_o, w_na_o, w_out,
              g_norm_ffn, w_up, w_conv, b_conv, w_down):
    xp = x_prompt
    xs = x_sample
    cond_ctx = c_ctx[None, :]
    new_k, new_v, new_sf, new_sb = [], [], [], []
    for l in range(DEPTH):
        lp = (w_ada[l], b_ada[l], g_norm_mix[l], w_in[l], w_alpha_fwd[l], b_alpha_fwd[l],
              w_alpha_bwd[l], b_alpha_bwd[l], g_gla_norm[l], g_q_norm[l], g_k_norm[l], rpb[l],
              w_gla_o[l], w_na_o[l], w_out[l], g_norm_ffn[l], w_up[l], w_conv[l], b_conv[l], w_down[l])
        xp, (kc, vc, sf, sb) = block(xp, cond_ctx, lp)
        new_k.append(kc)
        new_v.append(vc)
        new_sf.append(sf)
        new_sb.append(sb)
        xs, _ = block(xs, c, lp, (cache_na_k[:, l], cache_na_v[:, l], state_gla_fwd[:, l], state_gla_bwd[:, l]))
    new_cache_na_k = jnp.stack(new_k, axis=1)
    new_cache_na_v = jnp.stack(new_v, axis=1)
    new_state_gla_fwd = jnp.stack(new_sf, axis=1)
    new_state_gla_bwd = jnp.stack(new_sb, axis=1)
    return (xp, xs, new_cache_na_k, new_cache_na_v, new_state_gla_fwd, new_state_gla_bwd)
```

```python
import functools

import numpy as np
import jax
import jax.numpy as jnp
from jax import lax
from jax.experimental import pallas as pl
from jax.experimental.pallas import tpu as pltpu

F32 = jnp.float32
BF16 = jnp.bfloat16

D_MODEL = 1024
GRID_W = 64
GLA_HEADS, GLA_DK, GLA_DV = 4, 64, 128
GLA_QK = GLA_HEADS * GLA_DK
GLA_V = GLA_HEADS * GLA_DV
GLA_LOWRANK = 16
GLA_GATE_NORM = 16.0
GLA_CHUNK = 64
NA_HEADS, NA_HD = 8, 64
NA_W = NA_HEADS * NA_HD
NA_WIN_ROWS, NA_WIN_COLS = 8, 16
D_FF = 2816
EPS = 1e-6
IN_SIZES = (GLA_QK, GLA_QK, GLA_V, GLA_V, GLA_LOWRANK, GLA_LOWRANK, NA_W, NA_W, NA_W, D_MODEL, D_MODEL)

LANES = 128
ROW_TILE = 256
FF_TILE = 512
MASK_VALUE = -1e30
VMEM_LIMIT = 56 * 1024 * 1024

NT_DIMS = (((1,), (1,)), ((), ()))
TN_DIMS = (((0,), (0,)), ((), ()))


def _dot(a, b):
    return jnp.dot(a, b, preferred_element_type=F32)


def _dot_nt(a, b):
    return lax.dot_general(a, b, NT_DIMS, preferred_element_type=F32)


def _dot_tn(a, b):
    return lax.dot_general(a, b, TN_DIMS, preferred_element_type=F32)


def _split3(a):
    t1 = a.astype(BF16)
    e1 = a - t1.astype(F32)
    t2 = e1.astype(BF16)
    t3 = (e1 - t2.astype(F32)).astype(BF16)
    return t1, t2, t3


def _params(*sem):
    return pltpu.CompilerParams(dimension_semantics=sem, vmem_limit_bytes=VMEM_LIMIT)


def _const_spec(shape):
    zeros = (0,) * len(shape)
    return pl.BlockSpec(shape, lambda *_: zeros, pipeline_mode=pl.Buffered(1))


def _ada_kernel(c_ref, w_ref, b_ref, o_ref):
    c = c_ref[...]
    s = c * jax.nn.sigmoid(c)
    o_ref[...] = _dot(s.astype(BF16), w_ref[...].astype(BF16)) + b_ref[...]


def _ada(cond, w_ada, b_ada):
    n = cond.shape[0]
    return pl.pallas_call(
        _ada_kernel,
        grid=(6,),
        in_specs=[pl.BlockSpec((n, D_MODEL), lambda j: (0, 0)),
                  pl.BlockSpec((D_MODEL, D_MODEL), lambda j: (0, j)),
                  pl.BlockSpec((1, D_MODEL), lambda j: (0, j))],
        out_specs=pl.BlockSpec((n, D_MODEL), lambda j: (0, j)),
        out_shape=jax.ShapeDtypeStruct((n, 6 * D_MODEL), F32),
        compiler_params=_params("arbitrary"),
        name="ada",
    )(cond, w_ada, b_ada)


def _rms(x):
    return x * lax.rsqrt(jnp.mean(x * x, axis=-1, keepdims=True) + EPS)


PK_GLA = 0
PK_NA = PK_GLA + 2 * GLA_QK + 2 * GLA_V
PK_MERGE = PK_NA + 3 * NA_W
PK_LR = PK_MERGE + 2 * D_MODEL
PK_END = PK_LR + LANES


def _inproj_kernel(x_ref, mod_ref, g_ref, w_ref, wlr_ref, blr_ref, gqn_ref, gkn_ref, seg_ref,
                   q_ref, k_ref, v_ref, gr_ref, la_ref, nq_ref, nk_ref, nv_ref, sg_ref):
    h = _rms(x_ref[...]) * g_ref[...]
    h = (h * (1.0 + mod_ref[1]) + mod_ref[0]).astype(BF16)

    u = _dot(h, w_ref[:, PK_GLA:PK_NA])
    q_ref[...] = (u[:, :GLA_QK] * (GLA_DK ** -0.5)).astype(BF16)
    k_ref[...] = u[:, GLA_QK:2 * GLA_QK].astype(BF16)
    v_ref[...] = u[:, 2 * GLA_QK:2 * GLA_QK + GLA_V].astype(BF16)
    gr = u[:, 2 * GLA_QK + GLA_V:]
    gr_ref[...] = (gr * jax.nn.sigmoid(gr)).astype(BF16)

    z = _dot(_dot(h, w_ref[:, PK_LR:PK_END]).astype(BF16), wlr_ref[...]) + blr_ref[...]
    log_sig = jnp.minimum(z, 0.0) - jnp.log1p(jnp.exp(-jnp.abs(z)))
    la_ref[...] = log_sig * (1.0 / GLA_GATE_NORM)

    u = _dot(h, w_ref[:, PK_NA:PK_MERGE])

    def head_norm(a, gain):
        ssq = _dot((a * a).astype(BF16), seg_ref[...])
        return a * lax.rsqrt(ssq * (1.0 / NA_HD) + EPS) * gain

    nq_ref[...] = (head_norm(u[:, :NA_W], gqn_ref[...]) * (NA_HD ** -0.5)).astype(BF16)
    nk_ref[...] = head_norm(u[:, NA_W:2 * NA_W], gkn_ref[...])
    nv_ref[...] = u[:, 2 * NA_W:]

    sg_ref[...] = jax.nn.sigmoid(_dot(h, w_ref[:, PK_MERGE:PK_LR])).astype(BF16)


def _inproj(x, mod, mod_row, g_mix, w_pk, w_lr, b_lr, g_qn, g_kn, seg):
    n = x.shape[0]
    tm = ROW_TILE
    row = lambda c: pl.BlockSpec((tm, c), lambda i: (i, 0))
    outs = [(GLA_QK, BF16), (GLA_QK, BF16), (GLA_V, BF16), (GLA_V, BF16), (2 * GLA_QK, F32),
            (NA_W, BF16), (NA_W, F32), (NA_W, F32), (2 * D_MODEL, BF16)]
    return pl.pallas_call(
        _inproj_kernel,
        grid=(n // tm,),
        in_specs=[row(D_MODEL),
                  pl.BlockSpec((None, 6, 1, D_MODEL), lambda i: (mod_row(i * tm), 0, 0, 0)),
                  _const_spec(g_mix.shape), _const_spec(w_pk.shape), _const_spec(w_lr.shape),
                  _const_spec(b_lr.shape), _const_spec(g_qn.shape), _const_spec(g_kn.shape),
                  _const_spec(seg.shape)],
        out_specs=[row(c) for c, _ in outs],
        out_shape=[jax.ShapeDtypeStruct((n, c), dt) for c, dt in outs],
        compiler_params=_params("parallel"),
        name="inproj",
    )(x, mod, g_mix, w_pk, w_lr, b_lr, g_qn, g_kn, seg)


def _gla_kernel(has_s0, nchunks, *refs):
    qf, kf, vf, laf, qb, kb, vb, lab = refs[:8]
    if has_s0:
        s0f, s0b, of, ob, sf, sb, st = refs[8:]
    else:
        of, ob, sf, sb, st = refs[8:]
        s0f = s0b = None
    n = pl.program_id(1)
    bt = qf.shape[0]
    C, H, DK, DV = GLA_CHUNK, GLA_HEADS, GLA_DK, GLA_DV
    row = lax.broadcasted_iota(jnp.int32, (C, C), 0)
    col = lax.broadcasted_iota(jnp.int32, (C, C), 1)

    @pl.when(n == 0)
    def _init():
        for d, s0 in enumerate((s0f, s0b)):
            for b in range(bt):
                for h in range(H):
                    st[d, b, h] = s0[b, h].T if has_s0 else jnp.zeros((DV, DK), F32)

    for d, (q_ref, k_ref, v_ref, la_ref, o_ref) in enumerate(((qf, kf, vf, laf, of), (qb, kb, vb, lab, ob))):
        tri = (row >= col) if d == 0 else (row <= col)
        tri_m = jnp.where(tri, 1.0, 0.0).astype(BF16)
        for b in range(bt):
            l1, l2, l3 = _split3(la_ref[b])
            bc = _dot(tri_m, l1) + _dot(tri_m, l2) + _dot(tri_m, l3)
            tot = bc[C - 1:C, :] if d == 0 else bc[0:1, :]
            q = q_ref[b].astype(F32)
            k = k_ref[b].astype(F32)
            v = v_ref[b]
            qt = (q * jnp.exp(bc)).astype(BF16)
            kt = (k * jnp.exp(-bc)).astype(BF16)
            kh = (k * jnp.exp(tot - bc)).astype(BF16)
            et = jnp.exp(tot)
            outs = []
            for h in range(H):
                ks = slice(h * DK, (h + 1) * DK)
                vs = slice(h * DV, (h + 1) * DV)
                s = jnp.where(tri, _dot_nt(qt[:, ks], kt[:, ks]), 0.0).astype(BF16)
                s_prev = st[d, b, h]
                outs.append(_dot(s, v[:, vs]) + _dot_nt(qt[:, ks], s_prev.astype(BF16)))
                st[d, b, h] = et[:, ks] * s_prev + _dot_tn(v[:, vs], kh[:, ks])
            o_ref[b] = jnp.concatenate(outs, axis=-1).astype(o_ref.dtype)

    @pl.when(n == nchunks - 1)
    def _fin():
        for b in range(bt):
            for h in range(H):
                sf[b, h] = st[0, b, h].T
                sb[b, h] = st[1, b, h].T


def _gla(q, k, v, la, s0, batch, seq, bt):
    C = GLA_CHUNK
    nc = seq // C
    q, k, v, la = (a.reshape(batch, seq, a.shape[-1]) for a in (q, k, v, la))
    fwd = lambda c, j=0: pl.BlockSpec((bt, C, c), lambda i, n: (i, n, j))
    bwd = lambda c, j=0: pl.BlockSpec((bt, C, c), lambda i, n: (i, nc - 1 - n, j))
    st_spec = pl.BlockSpec((bt, GLA_HEADS, GLA_DK, GLA_DV), lambda i, n: (i, 0, 0, 0))
    in_specs = [fwd(GLA_QK), fwd(GLA_QK), fwd(GLA_V), fwd(GLA_QK, 0),
                bwd(GLA_QK), bwd(GLA_QK), bwd(GLA_V), bwd(GLA_QK, 1)]
    args = [q, k, v, la, q, k, v, la]
    if s0 is not None:
        in_specs += [st_spec, st_spec]
        args += list(s0)
    st_shape = jax.ShapeDtypeStruct((batch, GLA_HEADS, GLA_DK, GLA_DV), F32)
    o_shape = jax.ShapeDtypeStruct((batch, seq, GLA_V), BF16)
    of, ob, sf, sb = pl.pallas_call(
        functools.partial(_gla_kernel, s0 is not None, nc),
        grid=(batch // bt, nc),
        in_specs=in_specs,
        out_specs=[fwd(GLA_V), bwd(GLA_V), st_spec, st_spec],
        out_shape=[o_shape, o_shape, st_shape, st_shape],
        scratch_shapes=[pltpu.VMEM((2, bt, GLA_HEADS, GLA_DV, GLA_DK), F32)],
        compiler_params=_params("parallel", "arbitrary"),
        name="gla",
    )(*args)
    return of.reshape(batch * seq, GLA_V), ob.reshape(batch * seq, GLA_V), sf, sb


def _softmax_pv(scores, values):
    m = functools.reduce(jnp.maximum, [jnp.max(s, axis=-1, keepdims=True) for s in scores])
    ps = [jnp.exp(s - m) for s in scores]
    l = functools.reduce(jnp.add, [jnp.sum(p, axis=-1, keepdims=True) for p in ps])
    o = functools.reduce(jnp.add, [_dot(p.astype(BF16), v) for p, v in zip(ps, values)])
    return o / l


def _ctx_attn_kernel(q_ref, k_ref, v_ref, o_ref):
    q = q_ref[...]
    k = k_ref[...].astype(BF16)
    v = v_ref[...].astype(BF16)
    outs = []
    for h in range(NA_HEADS):
        hs = slice(h * NA_HD, (h + 1) * NA_HD)
        outs.append(_softmax_pv([_dot_nt(q[:, hs], k[:, hs])], [v[:, hs]]))
    o_ref[...] = jnp.concatenate(outs, axis=-1).astype(o_ref.dtype)


def _ctx_attn(q, k, v, batch, seq):
    spec = pl.BlockSpec((seq, NA_W), lambda b: (b, 0))
    return pl.pallas_call(
        _ctx_attn_kernel,
        grid=(batch,),
        in_specs=[spec, spec, spec],
        out_specs=spec,
        out_shape=jax.ShapeDtypeStruct((batch * seq, NA_W), BF16),
        compiler_params=_params("parallel"),
        name="ctx_attn",
    )(q, k, v)


N_DR = 2 * NA_WIN_ROWS - 1
N_DC = 2 * NA_WIN_COLS - 1
DC_PAD = 32
N_PAIR = N_DR - 1


def _bias_kernel(r_ref, o_ref):
    n = GRID_W * LANES
    lane = lax.broadcasted_iota(jnp.int32, (2 * DC_PAD, n), 1)
    kk = lax.broadcasted_iota(jnp.int32, (2 * DC_PAD, n), 0)
    qc = lane // LANES
    half = (lane % LANES) // GRID_W
    kc = lane % GRID_W
    dc = jnp.clip(kc - qc, -(NA_WIN_COLS - 1), NA_WIN_COLS - 1) + NA_WIN_COLS - 1
    onehot = jnp.where(kk == half * DC_PAD + dc, 1.0, 0.0).astype(BF16)
    r1, r2, r3 = _split3(r_ref[...])
    t = _dot(r1, onehot) + _dot(r2, onehot) + _dot(r3, onehot)
    lane = lax.broadcasted_iota(jnp.int32, t.shape, 1)
    qc = lane // LANES
    kc = lane % GRID_W
    start = jnp.clip(qc - NA_WIN_COLS // 2, 0, GRID_W - NA_WIN_COLS)
    in_win = (kc >= start) & (kc < start + NA_WIN_COLS)
    o_ref[...] = jnp.where(in_win, t, MASK_VALUE)


def _bias_table(rpb):
    pad = jnp.zeros((NA_HEADS, N_PAIR, DC_PAD - N_DC), F32)
    r = jnp.concatenate([rpb[:, :N_PAIR], pad, rpb[:, 1:], pad], axis=-1).reshape(NA_HEADS * N_PAIR, 2 * DC_PAD)
    t = pl.pallas_call(
        _bias_kernel,
        out_shape=jax.ShapeDtypeStruct((NA_HEADS * N_PAIR, GRID_W * LANES), F32),
        compiler_params=pltpu.CompilerParams(vmem_limit_bytes=VMEM_LIMIT),
        name="na_bias",
    )(r)
    return t.reshape(NA_HEADS, N_PAIR, GRID_W, LANES)


def _na_kernel(rows, q_ref, k_ref, v_ref, kc_ref, vc_ref, tbl_ref, o_ref):
    r = pl.program_id(1)
    wr = min(NA_WIN_ROWS, rows)
    r0 = jnp.clip(r - wr // 2, 0, rows - wr)
    dr0 = r0 - r + NA_WIN_ROWS - 1
    start = pl.multiple_of(r0 * GRID_W, GRID_W)
    q = q_ref[...]
    kl = k_ref[pl.ds(start, wr * GRID_W), :].astype(BF16)
    vl = v_ref[pl.ds(start, wr * GRID_W), :].astype(BF16)
    kc = kc_ref[...].astype(BF16)
    vc = vc_ref[...].astype(BF16)
    outs = []
    for h in range(NA_HEADS):
        hs = slice(h * NA_HD, (h + 1) * NA_HD)
        bias = jnp.concatenate([tbl_ref[h, dr0 + 2 * j] for j in range(wr // 2)], axis=-1)
        s_loc = _dot_nt(q[:, hs], kl[:, hs]) + bias
        s_ctx = _dot_nt(q[:, hs], kc[:, hs])
        outs.append(_softmax_pv([s_loc, s_ctx], [vl[:, hs], vc[:, hs]]))
    o_ref[...] = jnp.concatenate(outs, axis=-1).astype(o_ref.dtype)


def _na_attn(q, k, v, k_ctx, v_ctx, tbl, batch, seq):
    rows = seq // GRID_W
    assert rows >= NA_WIN_ROWS and NA_WIN_ROWS % 2 == 0
    past = k_ctx.shape[1]
    q, k, v = (a.reshape(batch, seq, NA_W) for a in (q, k, v))
    whole = lambda t: pl.BlockSpec((None, t, NA_W), lambda b, r: (b, 0, 0))
    tile = pl.BlockSpec((None, GRID_W, NA_W), lambda b, r: (b, r, 0))
    o = pl.pallas_call(
        functools.partial(_na_kernel, rows),
        grid=(batch, rows),
        in_specs=[tile, whole(seq), whole(seq), whole(past), whole(past), _const_spec(tbl.shape)],
        out_specs=tile,
        out_shape=jax.ShapeDtypeStruct((batch, seq, NA_W), BF16),
        compiler_params=_params("parallel", "arbitrary"),
        name="na_attn",
    )(q, k, v, k_ctx, v_ctx, tbl)
    return o.reshape(batch * seq, NA_W)


def _merge_kernel(x_ref, mod_ref, of_ref, ob_ref, gr_ref, na_ref, sg_ref, ggla_ref, wg_ref, wn_ref, wo_ref,
                  gffn_ref, x1_ref, h2_ref):
    o = of_ref[...].astype(F32) + ob_ref[...].astype(F32)
    on = jnp.concatenate([_rms(o[:, h * GLA_DV:(h + 1) * GLA_DV]) for h in range(GLA_HEADS)], axis=-1)
    on = on * ggla_ref[...] * gr_ref[...].astype(F32)
    b_gla = _dot(on.astype(BF16), wg_ref[...])
    b_na = _dot(na_ref[...], wn_ref[...])
    sg = sg_ref[...].astype(F32)
    mix = _dot((sg[:, :D_MODEL] * b_gla + sg[:, D_MODEL:] * b_na).astype(BF16), wo_ref[...])
    x1 = x_ref[...] + mod_ref[2] * mix
    x1_ref[...] = x1
    h2_ref[...] = (_rms(x1) * gffn_ref[...] * (1.0 + mod_ref[4]) + mod_ref[3]).astype(BF16)


def _merge(x, mod, mod_row, of, ob, gr, na, sg, g_gla, w_gla_o, w_na_o, w_out, g_ffn):
    n = x.shape[0]
    tm = ROW_TILE
    row = lambda c: pl.BlockSpec((tm, c), lambda i: (i, 0))
    return pl.pallas_call(
        _merge_kernel,
        grid=(n // tm,),
        in_specs=[row(D_MODEL),
                  pl.BlockSpec((None, 6, 1, D_MODEL), lambda i: (mod_row(i * tm), 0, 0, 0)),
                  row(GLA_V), row(GLA_V), row(GLA_V), row(NA_W), row(2 * D_MODEL),
                  _const_spec(g_gla.shape), _const_spec(w_gla_o.shape), _const_spec(w_na_o.shape),
                  _const_spec(w_out.shape), _const_spec(g_ffn.shape)],
        out_specs=[row(D_MODEL), row(D_MODEL)],
        out_shape=[jax.ShapeDtypeStruct((n, D_MODEL), F32), jax.ShapeDtypeStruct((n, D_MODEL), BF16)],
        compiler_params=_params("parallel"),
        name="merge",
    )(x, mod, of, ob, gr, na, sg, g_gla, w_gla_o, w_na_o, w_out, g_ffn)


def _ffn_kernel(h_ref, x1_ref, mod_ref, wup_ref, wc_ref, bc_ref, wdn_ref, o_ref, acc_ref):
    t = h_ref.shape[0]
    h = h_ref[...]
    rows = lax.broadcasted_iota(jnp.int32, (t, 1), 0)
    first = rows == 0
    last = rows == t - 1

    def conv(u, lo, hi):
        prev = jnp.where(first, 0.0, pltpu.roll(u, 1, axis=0))
        nxt = jnp.where(last, 0.0, pltpu.roll(u, t - 1, axis=0))
        return wc_ref[0:1, lo:hi] * prev + wc_ref[1:2, lo:hi] * u + wc_ref[2:3, lo:hi] * nxt + bc_ref[:, lo:hi]

    for j, lo in enumerate(range(0, D_FF, FF_TILE)):
        hi = min(lo + FF_TILE, D_FF)
        a = conv(_dot(h, wup_ref[:, lo:hi]), lo, hi)
        g = conv(_dot(h, wup_ref[:, D_FF + lo:D_FF + hi]), D_FF + lo, D_FF + hi)
        part = _dot((a * (g * jax.nn.sigmoid(g))).astype(BF16), wdn_ref[lo:hi, :])
        if j == 0:
            acc_ref[...] = part
        else:
            acc_ref[...] += part
    o_ref[...] = x1_ref[...] + mod_ref[5] * acc_ref[...]


def _ffn(h2, x1, mod, mod_row, w_up, w_conv, b_conv, w_down, batch, seq):
    blk = lambda: pl.BlockSpec((seq, D_MODEL), lambda b: (b, 0))
    return pl.pallas_call(
        _ffn_kernel,
        grid=(batch,),
        in_specs=[blk(), blk(),
                  pl.BlockSpec((None, 6, 1, D_MODEL), lambda b: (mod_row(b * seq), 0, 0, 0)),
                  _const_spec(w_up.shape), _const_spec(w_conv.shape), _const_spec(b_conv.shape),
                  _const_spec(w_down.shape)],
        out_specs=blk(),
        out_shape=jax.ShapeDtypeStruct((batch * seq, D_MODEL), F32),
        scratch_shapes=[pltpu.VMEM((seq, D_MODEL), F32)],
        compiler_params=_params("parallel"),
        name="conv_ffn",
    )(h2, x1, mod, w_up, w_conv, b_conv, w_down)


def _layer(x, mod, mod_row, wts, ctx, gla_bt):
    batch, seq, _ = x.shape
    x2 = x.reshape(batch * seq, D_MODEL)
    q, k, v, gr, la, nq, nk, nv, sg = _inproj(x2, mod, mod_row, wts["g_mix"], wts["w_pk"], wts["w_lr"],
                                              wts["b_lr"], wts["g_qn"], wts["g_kn"], wts["seg"])
    s0 = None if ctx is None else (ctx[2], ctx[3])
    of, ob, sf, sb = _gla(q, k, v, la, s0, batch, seq, gla_bt)
    if ctx is None:
        o_na = _ctx_attn(nq, nk, nv, batch, seq)
    else:
        o_na = _na_attn(nq, nk, nv, ctx[0], ctx[1], ctx[4], batch, seq)
    x1, h2 = _merge(x2, mod, mod_row, of, ob, gr, o_na, sg, wts["g_gla"], wts["w_gla_o"], wts["w_na_o"],
                    wts["w_out"], wts["g_ffn"])
    y = _ffn(h2, x1, mod, mod_row, wts["w_up"], wts["w_conv"], wts["b_conv"], wts["w_down"], batch, seq)
    return y.reshape(batch, seq, D_MODEL), nk, nv, sf, sb


def _pack_weights(l, w_in, w_alpha_fwd, b_alpha_fwd, w_alpha_bwd, b_alpha_bwd, g_norm_mix, g_gla_norm,
                  g_q_norm, g_k_norm, w_gla_o, w_na_o, w_out, g_norm_ffn, w_up, w_conv, b_conv, w_down):
    off = np.cumsum((0,) + IN_SIZES)
    w = w_in[l]
    w_pk = jnp.concatenate(
        [w[:, off[0]:off[4]], w[:, off[6]:off[11]], w[:, off[4]:off[6]],
         jnp.zeros((D_MODEL, LANES - 2 * GLA_LOWRANK), F32)], axis=1).astype(BF16)
    w_lr = jnp.zeros((LANES, 2 * GLA_QK), F32)
    w_lr = w_lr.at[:GLA_LOWRANK, :GLA_QK].set(w_alpha_fwd[l])
    w_lr = w_lr.at[GLA_LOWRANK:2 * GLA_LOWRANK, GLA_QK:].set(w_alpha_bwd[l]).astype(BF16)
    seg = jnp.asarray(np.kron(np.eye(NA_HEADS), np.ones((NA_HD, NA_HD))), BF16)
    return dict(
        g_mix=g_norm_mix[l][None], w_pk=w_pk, w_lr=w_lr,
        b_lr=jnp.concatenate([b_alpha_fwd[l], b_alpha_bwd[l]])[None],
        g_qn=jnp.tile(g_q_norm[l], NA_HEADS)[None], g_kn=jnp.tile(g_k_norm[l], NA_HEADS)[None], seg=seg,
        g_gla=jnp.tile(g_gla_norm[l], GLA_HEADS)[None],
        w_gla_o=w_gla_o[l].astype(BF16), w_na_o=w_na_o[l].astype(BF16), w_out=w_out[l].astype(BF16),
        g_ffn=g_norm_ffn[l][None], w_up=w_up[l].astype(BF16), w_conv=w_conv[l], b_conv=b_conv[l][None],
        w_down=w_down[l].astype(BF16))


def kernel(x_prompt, x_sample, c, cache_na_k, cache_na_v, state_gla_fwd, state_gla_bwd, c_ctx, w_ada, b_ada,
           g_norm_mix, w_in, w_alpha_fwd, b_alpha_fwd, w_alpha_bwd, b_alpha_bwd, g_gla_norm, g_q_norm, g_k_norm,
           rpb, w_gla_o, w_na_o, w_out, g_norm_ffn, w_up, w_conv, b_conv, w_down):
    depth = w_in.shape[0]
    batch, seq, _ = x_prompt.shape
    dec_batch, dec_seq, _ = x_sample.shape
    past = cache_na_k.shape[2]
    cond = jnp.concatenate([c_ctx[None], c], axis=0)
    xp, xs = x_prompt, x_sample
    new_k, new_v, new_sf, new_sb = [], [], [], []
    for l in range(depth):
        wts = _pack_weights(l, w_in, w_alpha_fwd, b_alpha_fwd, w_alpha_bwd, b_alpha_bwd, g_norm_mix, g_gla_norm,
                            g_q_norm, g_k_norm, w_gla_o, w_na_o, w_out, g_norm_ffn, w_up, w_conv, b_conv, w_down)
        mod = _ada(cond, w_ada[l], b_ada[l][None]).reshape(1 + dec_batch, 6, 1, D_MODEL)
        tbl = _bias_table(rpb[l])
        xp, kc, vc, sf, sb = _layer(xp, mod, lambda tok: 0, wts, None, gla_bt=4)
        new_k.append(kc.reshape(batch, seq, NA_HEADS, NA_HD))
        new_v.append(vc.reshape(batch, seq, NA_HEADS, NA_HD))
        new_sf.append(sf)
        new_sb.append(sb)
        ctx = (cache_na_k[:, l].reshape(dec_batch, past, NA_W), cache_na_v[:, l].reshape(dec_batch, past, NA_W),
               state_gla_fwd[:, l], state_gla_bwd[:, l], tbl)
        xs, _, _, _, _ = _layer(xs, mod, lambda tok: 1 + tok // dec_seq, wts, ctx, gla_bt=dec_batch)
    return (xp, xs, jnp.stack(new_k, axis=1), jnp.stack(new_v, axis=1),
            jnp.stack(new_sf, axis=1), jnp.stack(new_sb, axis=1))
```

```python
import functools

import numpy as np
import jax
import jax.numpy as jnp
from jax import lax
from jax.experimental import pallas as pl
from jax.experimental.pallas import tpu as pltpu

F32 = jnp.float32
BF16 = jnp.bfloat16

D_MODEL = 1024
GRID_W = 64
GLA_HEADS, GLA_DK, GLA_DV = 4, 64, 128
GLA_QK = GLA_HEADS * GLA_DK
GLA_V = GLA_HEADS * GLA_DV
GLA_LOWRANK = 16
GLA_GATE_NORM = 16.0
GLA_CHUNK = 64
NA_HEADS, NA_HD = 8, 64
NA_W = NA_HEADS * NA_HD
NA_WIN_ROWS, NA_WIN_COLS = 8, 16
D_FF = 2816
EPS = 1e-6
IN_SIZES = (GLA_QK, GLA_QK, GLA_V, GLA_V, GLA_LOWRANK, GLA_LOWRANK, NA_W, NA_W, NA_W, D_MODEL, D_MODEL)

LANES = 128
ROW_TILE = 256
FF_TILE = 512
MASK_VALUE = -1e30
VMEM_LIMIT = 56 * 1024 * 1024

NT_DIMS = (((1,), (1,)), ((), ()))
TN_DIMS = (((0,), (0,)), ((), ()))


def _dot(a, b):
    return jnp.dot(a, b, preferred_element_type=F32)


def _dot_nt(a, b):
    return lax.dot_general(a, b, NT_DIMS, preferred_element_type=F32)


def _dot_tn(a, b):
    return lax.dot_general(a, b, TN_DIMS, preferred_element_type=F32)


def _split3(a):
    t1 = a.astype(BF16)
    e1 = a - t1.astype(F32)
    t2 = e1.astype(BF16)
    t3 = (e1 - t2.astype(F32)).astype(BF16)
    return t1, t2, t3


def _params(*sem):
    return pltpu.CompilerParams(dimension_semantics=sem, vmem_limit_bytes=VMEM_LIMIT)


def _const_spec(shape):
    zeros = (0,) * len(shape)
    return pl.BlockSpec(shape, lambda *_: zeros, pipeline_mode=pl.Buffered(1))


def _ada_kernel(c_ref, w_ref, b_ref, o_ref):
    c = c_ref[...]
    s = c * jax.nn.sigmoid(c)
    o_ref[...] = _dot(s.astype(BF16), w_ref[...].astype(BF16)) + b_ref[...]


def _ada(cond, w_ada, b_ada):
    n = cond.shape[0]
    return pl.pallas_call(
        _ada_kernel,
        grid=(6,),
        in_specs=[pl.BlockSpec((n, D_MODEL), lambda j: (0, 0)),
                  pl.BlockSpec((D_MODEL, D_MODEL), lambda j: (0, j)),
                  pl.BlockSpec((1, D_MODEL), lambda j: (0, j))],
        out_specs=pl.BlockSpec((n, D_MODEL), lambda j: (0, j)),
        out_shape=jax.ShapeDtypeStruct((n, 6 * D_MODEL), F32),
        compiler_params=_params("arbitrary"),
        name="ada",
    )(cond, w_ada, b_ada)


def _rms(x):
    return x * lax.rsqrt(jnp.mean(x * x, axis=-1, keepdims=True) + EPS)


PK_GLA = 0
PK_NA = PK_GLA + 2 * GLA_QK + 2 * GLA_V
PK_MERGE = PK_NA + 3 * NA_W
PK_LR = PK_MERGE + 2 * D_MODEL
PK_END = PK_LR + LANES


PACK_ROWS = 128


def _pack_kernel(w_ref, o_ref):
    off = np.cumsum((0,) + IN_SIZES)
    lr0, lr1 = int(off[4]), int(off[6])
    skew = lr1 - lr0
    o_ref[:, :lr0] = w_ref[:, :lr0].astype(BF16)
    n_rest = int(off[-1]) - lr1
    for c in range(0, n_rest, 4 * LANES):
        width = min(4 * LANES, n_rest - c)
        src = w_ref[:, lr0 + c:min(lr0 + c + width + LANES, int(off[-1]))]
        o_ref[:, PK_NA + c:PK_NA + c + width] = src[:, skew:skew + width].astype(BF16)
    lr = w_ref[:, lr0:lr0 + LANES]
    lane = lax.broadcasted_iota(jnp.int32, lr.shape, 1)
    o_ref[:, PK_LR:PK_END] = jnp.where(lane < skew, lr, 0.0).astype(BF16)


def _pack_w_in(w):
    assert IN_SIZES[4] + IN_SIZES[5] <= LANES and sum(IN_SIZES[:4]) % LANES == 0
    return pl.pallas_call(
        _pack_kernel,
        grid=(D_MODEL // PACK_ROWS,),
        in_specs=[pl.BlockSpec((PACK_ROWS, w.shape[1]), lambda i: (i, 0))],
        out_specs=pl.BlockSpec((PACK_ROWS, PK_END), lambda i: (i, 0)),
        out_shape=jax.ShapeDtypeStruct((D_MODEL, PK_END), BF16),
        compiler_params=_params("parallel"),
        name="pack_w_in",
    )(w)


def _inproj_kernel(x_ref, mod_ref, g_ref, w_ref, wlr_ref, blr_ref, gqn_ref, gkn_ref, seg_ref,
                   q_ref, k_ref, v_ref, gr_ref, la_ref, nq_ref, nk_ref, nv_ref, sg_ref):
    h = _rms(x_ref[...]) * g_ref[...]
    h = (h * (1.0 + mod_ref[1]) + mod_ref[0]).astype(BF16)

    u = _dot(h, w_ref[:, PK_GLA:PK_NA])
    q_ref[...] = (u[:, :GLA_QK] * (GLA_DK ** -0.5)).astype(BF16)
    k_ref[...] = u[:, GLA_QK:2 * GLA_QK].astype(BF16)
    v_ref[...] = u[:, 2 * GLA_QK:2 * GLA_QK + GLA_V].astype(BF16)
    gr = u[:, 2 * GLA_QK + GLA_V:]
    gr_ref[...] = (gr * jax.nn.sigmoid(gr)).astype(BF16)

    z = _dot(_dot(h, w_ref[:, PK_LR:PK_END]).astype(BF16), wlr_ref[...]) + blr_ref[...]
    log_sig = jnp.minimum(z, 0.0) - jnp.log1p(jnp.exp(-jnp.abs(z)))
    la_ref[...] = log_sig * (1.0 / GLA_GATE_NORM)

    u = _dot(h, w_ref[:, PK_NA:PK_MERGE])

    def head_norm(a, gain):
        ssq = _dot((a * a).astype(BF16), seg_ref[...])
        return a * lax.rsqrt(ssq * (1.0 / NA_HD) + EPS) * gain

    nq_ref[...] = (head_norm(u[:, :NA_W], gqn_ref[...]) * (NA_HD ** -0.5)).astype(BF16)
    nk_ref[...] = head_norm(u[:, NA_W:2 * NA_W], gkn_ref[...])
    nv_ref[...] = u[:, 2 * NA_W:]

    sg_ref[...] = jax.nn.sigmoid(_dot(h, w_ref[:, PK_MERGE:PK_LR])).astype(BF16)


def _inproj(x, mod, mod_row, g_mix, w_pk, w_lr, b_lr, g_qn, g_kn, seg):
    n = x.shape[0]
    tm = ROW_TILE
    row = lambda c: pl.BlockSpec((tm, c), lambda i: (i, 0))
    outs = [(GLA_QK, BF16), (GLA_QK, BF16), (GLA_V, BF16), (GLA_V, BF16), (2 * GLA_QK, F32),
            (NA_W, BF16), (NA_W, F32), (NA_W, F32), (2 * D_MODEL, BF16)]
    return pl.pallas_call(
        _inproj_kernel,
        grid=(n // tm,),
        in_specs=[row(D_MODEL),
                  pl.BlockSpec((None, 6, 1, D_MODEL), lambda i: (mod_row(i * tm), 0, 0, 0)),
                  _const_spec(g_mix.shape), _const_spec(w_pk.shape), _const_spec(w_lr.shape),
                  _const_spec(b_lr.shape), _const_spec(g_qn.shape), _const_spec(g_kn.shape),
                  _const_spec(seg.shape)],
        out_specs=[row(c) for c, _ in outs],
        out_shape=[jax.ShapeDtypeStruct((n, c), dt) for c, dt in outs],
        compiler_params=_params("parallel"),
        name="inproj",
    )(x, mod, g_mix, w_pk, w_lr, b_lr, g_qn, g_kn, seg)


def _gla_kernel(has_s0, nchunks, *refs):
    qf, kf, vf, laf, qb, kb, vb, lab = refs[:8]
    if has_s0:
        s0f, s0b, of, ob, sf, sb, st = refs[8:]
    else:
        of, ob, sf, sb, st = refs[8:]
        s0f = s0b = None
    n = pl.program_id(1)
    bt = qf.shape[0]
    C, H, DK, DV = GLA_CHUNK, GLA_HEADS, GLA_DK, GLA_DV
    row = lax.broadcasted_iota(jnp.int32, (C, C), 0)
    col = lax.broadcasted_iota(jnp.int32, (C, C), 1)

    @pl.when(n == 0)
    def _init():
        for d, s0 in enumerate((s0f, s0b)):
            for b in range(bt):
                for h in range(H):
                    st[d, b, h] = s0[b, h].T if has_s0 else jnp.zeros((DV, DK), F32)

    for d, (q_ref, k_ref, v_ref, la_ref, o_ref) in enumerate(((qf, kf, vf, laf, of), (qb, kb, vb, lab, ob))):
        tri = (row >= col) if d == 0 else (row <= col)
        tri_m = jnp.where(tri, 1.0, 0.0).astype(BF16)
        for b in range(bt):
            l1, l2, l3 = _split3(la_ref[b])
            bc = _dot(tri_m, l1) + _dot(tri_m, l2) + _dot(tri_m, l3)
            tot = bc[C - 1:C, :] if d == 0 else bc[0:1, :]
            q = q_ref[b].astype(F32)
            k = k_ref[b].astype(F32)
            v = v_ref[b]
            qt = (q * jnp.exp(bc)).astype(BF16)
            kt = (k * jnp.exp(-bc)).astype(BF16)
            kh = (k * jnp.exp(tot - bc)).astype(BF16)
            et = jnp.exp(tot)
            outs = []
            for h in range(H):
                ks = slice(h * DK, (h + 1) * DK)
                vs = slice(h * DV, (h + 1) * DV)
                s = jnp.where(tri, _dot_nt(qt[:, ks], kt[:, ks]), 0.0).astype(BF16)
                s_prev = st[d, b, h]
                outs.append(_dot(s, v[:, vs]) + _dot_nt(qt[:, ks], s_prev.astype(BF16)))
                st[d, b, h] = et[:, ks] * s_prev + _dot_tn(v[:, vs], kh[:, ks])
            o_ref[b] = jnp.concatenate(outs, axis=-1).astype(o_ref.dtype)

    @pl.when(n == nchunks - 1)
    def _fin():
        for b in range(bt):
            for h in range(H):
                sf[b, h] = st[0, b, h].T
                sb[b, h] = st[1, b, h].T


def _gla(q, k, v, la, s0, batch, seq, bt):
    C = GLA_CHUNK
    nc = seq // C
    q, k, v, la = (a.reshape(batch, seq, a.shape[-1]) for a in (q, k, v, la))
    fwd = lambda c, j=0: pl.BlockSpec((bt, C, c), lambda i, n: (i, n, j))
    bwd = lambda c, j=0: pl.BlockSpec((bt, C, c), lambda i, n: (i, nc - 1 - n, j))
    st_spec = pl.BlockSpec((bt, GLA_HEADS, GLA_DK, GLA_DV), lambda i, n: (i, 0, 0, 0))
    in_specs = [fwd(GLA_QK), fwd(GLA_QK), fwd(GLA_V), fwd(GLA_QK, 0),
                bwd(GLA_QK), bwd(GLA_QK), bwd(GLA_V), bwd(GLA_QK, 1)]
    args = [q, k, v, la, q, k, v, la]
    if s0 is not None:
        in_specs += [st_spec, st_spec]
        args += list(s0)
    st_shape = jax.ShapeDtypeStruct((batch, GLA_HEADS, GLA_DK, GLA_DV), F32)
    o_shape = jax.ShapeDtypeStruct((batch, seq, GLA_V), BF16)
    of, ob, sf, sb = pl.pallas_call(
        functools.partial(_gla_kernel, s0 is not None, nc),
        grid=(batch // bt, nc),
        in_specs=in_specs,
        out_specs=[fwd(GLA_V), bwd(GLA_V), st_spec, st_spec],
        out_shape=[o_shape, o_shape, st_shape, st_shape],
        scratch_shapes=[pltpu.VMEM((2, bt, GLA_HEADS, GLA_DV, GLA_DK), F32)],
        compiler_params=_params("parallel", "arbitrary"),
        name="gla",
    )(*args)
    return of.reshape(batch * seq, GLA_V), ob.reshape(batch * seq, GLA_V), sf, sb


def _softmax_pv(scores, values):
    m = functools.reduce(jnp.maximum, [jnp.max(s, axis=-1, keepdims=True) for s in scores])
    ps = [jnp.exp(s - m) for s in scores]
    l = functools.reduce(jnp.add, [jnp.sum(p, axis=-1, keepdims=True) for p in ps])
    o = functools.reduce(jnp.add, [_dot(p.astype(BF16), v) for p, v in zip(ps, values)])
    return o / l


def _ctx_attn_kernel(q_ref, k_ref, v_ref, o_ref):
    q = q_ref[...]
    k = k_ref[...].astype(BF16)
    v = v_ref[...].astype(BF16)
    outs = []
    for h in range(NA_HEADS):
        hs = slice(h * NA_HD, (h + 1) * NA_HD)
        outs.append(_softmax_pv([_dot_nt(q[:, hs], k[:, hs])], [v[:, hs]]))
    o_ref[...] = jnp.concatenate(outs, axis=-1).astype(o_ref.dtype)


def _ctx_attn(q, k, v, batch, seq):
    spec = pl.BlockSpec((seq, NA_W), lambda b: (b, 0))
    return pl.pallas_call(
        _ctx_attn_kernel,
        grid=(batch,),
        in_specs=[spec, spec, spec],
        out_specs=spec,
        out_shape=jax.ShapeDtypeStruct((batch * seq, NA_W), BF16),
        compiler_params=_params("parallel"),
        name="ctx_attn",
    )(q, k, v)


N_DR = 2 * NA_WIN_ROWS - 1
N_DC = 2 * NA_WIN_COLS - 1
DC_PAD = 32
N_PAIR = N_DR - 1


def _bias_kernel(r_ref, o_ref):
    qc = lax.broadcasted_iota(jnp.int32, (GRID_W, LANES), 0)
    kc = lax.broadcasted_iota(jnp.int32, (GRID_W, LANES), 1) % GRID_W
    start = jnp.clip(qc - NA_WIN_COLS // 2, 0, GRID_W - NA_WIN_COLS)
    in_win = (kc >= start) & (kc < start + NA_WIN_COLS)
    shift = LANES - (NA_WIN_COLS - 1)
    for p in range(o_ref.shape[0]):
        row = jnp.broadcast_to(r_ref[p:p + 1, :], (GRID_W, LANES))
        tile = pltpu.roll(row, shift, axis=1, stride=1, stride_axis=0)
        o_ref[p] = jnp.where(in_win, tile, MASK_VALUE)


def _bias_table(rpb):
    assert 2 * GRID_W == LANES
    pad = jnp.zeros((NA_HEADS, N_PAIR, GRID_W - N_DC), F32)
    r = jnp.concatenate([rpb[:, :N_PAIR], pad, rpb[:, 1:], pad], axis=-1).reshape(NA_HEADS * N_PAIR, LANES)
    t = pl.pallas_call(
        _bias_kernel,
        out_shape=jax.ShapeDtypeStruct((NA_HEADS * N_PAIR, GRID_W, LANES), F32),
        compiler_params=pltpu.CompilerParams(vmem_limit_bytes=VMEM_LIMIT),
        name="na_bias",
    )(r)
    return t.reshape(NA_HEADS, N_PAIR, GRID_W, LANES)


def _na_kernel(rows, q_ref, k_ref, v_ref, kc_ref, vc_ref, tbl_ref, o_ref):
    r = pl.program_id(1)
    wr = min(NA_WIN_ROWS, rows)
    r0 = jnp.clip(r - wr // 2, 0, rows - wr)
    dr0 = r0 - r + NA_WIN_ROWS - 1
    start = pl.multiple_of(r0 * GRID_W, GRID_W)
    q = q_ref[...]
    kl = k_ref[pl.ds(start, wr * GRID_W), :].astype(BF16)
    vl = v_ref[pl.ds(start, wr * GRID_W), :].astype(BF16)
    kc = kc_ref[...].astype(BF16)
    vc = vc_ref[...].astype(BF16)
    outs = []
    for h in range(NA_HEADS):
        hs = slice(h * NA_HD, (h + 1) * NA_HD)
        bias = jnp.concatenate([tbl_ref[h, dr0 + 2 * j] for j in range(wr // 2)], axis=-1)
        s_loc = _dot_nt(q[:, hs], kl[:, hs]) + bias
        s_ctx = _dot_nt(q[:, hs], kc[:, hs])
        outs.append(_softmax_pv([s_loc, s_ctx], [vl[:, hs], vc[:, hs]]))
    o_ref[...] = jnp.concatenate(outs, axis=-1).astype(o_ref.dtype)


def _na_attn(q, k, v, k_ctx, v_ctx, tbl, batch, seq):
    rows = seq // GRID_W
    assert rows >= NA_WIN_ROWS and NA_WIN_ROWS % 2 == 0
    past = k_ctx.shape[1]
    q, k, v = (a.reshape(batch, seq, NA_W) for a in (q, k, v))
    whole = lambda t: pl.BlockSpec((None, t, NA_W), lambda b, r: (b, 0, 0))
    tile = pl.BlockSpec((None, GRID_W, NA_W), lambda b, r: (b, r, 0))
    o = pl.pallas_call(
        functools.partial(_na_kernel, rows),
        grid=(batch, rows),
        in_specs=[tile, whole(seq), whole(seq), whole(past), whole(past), _const_spec(tbl.shape)],
        out_specs=tile,
        out_shape=jax.ShapeDtypeStruct((batch, seq, NA_W), BF16),
        compiler_params=_params("parallel", "arbitrary"),
        name="na_attn",
    )(q, k, v, k_ctx, v_ctx, tbl)
    return o.reshape(batch * seq, NA_W)


def _merge_kernel(x_ref, mod_ref, of_ref, ob_ref, gr_ref, na_ref, sg_ref, ggla_ref, wg_ref, wn_ref, wo_ref,
                  gffn_ref, x1_ref, h2_ref):
    o = of_ref[...].astype(F32) + ob_ref[...].astype(F32)
    on = jnp.concatenate([_rms(o[:, h * GLA_DV:(h + 1) * GLA_DV]) for h in range(GLA_HEADS)], axis=-1)
    on = on * ggla_ref[...] * gr_ref[...].astype(F32)
    b_gla = _dot(on.astype(BF16), wg_ref[...])
    b_na = _dot(na_ref[...], wn_ref[...])
    sg = sg_ref[...].astype(F32)
    mix = _dot((sg[:, :D_MODEL] * b_gla + sg[:, D_MODEL:] * b_na).astype(BF16), wo_ref[...])
    x1 = x_ref[...] + mod_ref[2] * mix
    x1_ref[...] = x1
    h2_ref[...] = (_rms(x1) * gffn_ref[...] * (1.0 + mod_ref[4]) + mod_ref[3]).astype(BF16)


def _merge(x, mod, mod_row, of, ob, gr, na, sg, g_gla, w_gla_o, w_na_o, w_out, g_ffn):
    n = x.shape[0]
    tm = ROW_TILE
    row = lambda c: pl.BlockSpec((tm, c), lambda i: (i, 0))
    return pl.pallas_call(
        _merge_kernel,
        grid=(n // tm,),
        in_specs=[row(D_MODEL),
                  pl.BlockSpec((None, 6, 1, D_MODEL), lambda i: (mod_row(i * tm), 0, 0, 0)),
                  row(GLA_V), row(GLA_V), row(GLA_V), row(NA_W), row(2 * D_MODEL),
                  _const_spec(g_gla.shape), _const_spec(w_gla_o.shape), _const_spec(w_na_o.shape),
                  _const_spec(w_out.shape), _const_spec(g_ffn.shape)],
        out_specs=[row(D_MODEL), row(D_MODEL)],
        out_shape=[jax.ShapeDtypeStruct((n, D_MODEL), F32), jax.ShapeDtypeStruct((n, D_MODEL), BF16)],
        compiler_params=_params("parallel"),
        name="merge",
    )(x, mod, of, ob, gr, na, sg, g_gla, w_gla_o, w_na_o, w_out, g_ffn)


def _ffn_kernel(seq, h_ref, x1_ref, mod_ref, wup_ref, wc_ref, bc_ref, wdn_ref, o_ref, a_ref, g_ref):
    t = h_ref.shape[0]
    h = h_ref[...]
    pos = lax.broadcasted_iota(jnp.int32, (t, 1), 0) % seq
    first = pos == 0
    last = pos == seq - 1
    tiles = [(lo, min(lo + FF_TILE, D_FF)) for lo in range(0, D_FF, FF_TILE)]

    def up(j):
        lo, hi = tiles[j]
        a_ref[j % 2, :, :hi - lo] = _dot(h, wup_ref[:, lo:hi])
        g_ref[j % 2, :, :hi - lo] = _dot(h, wup_ref[:, D_FF + lo:D_FF + hi])

    def conv(u, lo, hi):
        prev = jnp.where(first, 0.0, pltpu.roll(u, 1, axis=0))
        nxt = jnp.where(last, 0.0, pltpu.roll(u, t - 1, axis=0))
        return wc_ref[0:1, lo:hi] * prev + wc_ref[1:2, lo:hi] * u + wc_ref[2:3, lo:hi] * nxt + bc_ref[:, lo:hi]

    up(0)
    for j, (lo, hi) in enumerate(tiles):
        if j + 1 < len(tiles):
            up(j + 1)
        a = conv(a_ref[j % 2, :, :hi - lo], lo, hi)
        g = conv(g_ref[j % 2, :, :hi - lo], D_FF + lo, D_FF + hi)
        part = _dot((a * (g * jax.nn.sigmoid(g))).astype(BF16), wdn_ref[lo:hi, :])
        if j == 0:
            o_ref[...] = part
        else:
            o_ref[...] += part
    o_ref[...] = x1_ref[...] + mod_ref[5] * o_ref[...]


def _ffn(h2, x1, mod, mod_row, w_up, w_conv, b_conv, w_down, batch, seq, seqs_per_step):
    rows = seq * seqs_per_step
    blk = lambda: pl.BlockSpec((rows, D_MODEL), lambda b: (b, 0))
    return pl.pallas_call(
        functools.partial(_ffn_kernel, seq),
        grid=(batch // seqs_per_step,),
        in_specs=[blk(), blk(),
                  pl.BlockSpec((None, 6, 1, D_MODEL), lambda b: (mod_row(b * rows), 0, 0, 0)),
                  _const_spec(w_up.shape), _const_spec(w_conv.shape), _const_spec(b_conv.shape),
                  _const_spec(w_down.shape)],
        out_specs=blk(),
        out_shape=jax.ShapeDtypeStruct((batch * seq, D_MODEL), F32),
        scratch_shapes=[pltpu.VMEM((2, rows, FF_TILE), F32), pltpu.VMEM((2, rows, FF_TILE), F32)],
        compiler_params=_params("parallel"),
        name="conv_ffn",
    )(h2, x1, mod, w_up, w_conv, b_conv, w_down)


def _layer(x, mod, mod_row, wts, ctx, gla_bt, ffn_seqs):
    batch, seq, _ = x.shape
    x2 = x.reshape(batch * seq, D_MODEL)
    q, k, v, gr, la, nq, nk, nv, sg = _inproj(x2, mod, mod_row, wts["g_mix"], wts["w_pk"], wts["w_lr"],
                                              wts["b_lr"], wts["g_qn"], wts["g_kn"], wts["seg"])
    s0 = None if ctx is None else (ctx[2], ctx[3])
    of, ob, sf, sb = _gla(q, k, v, la, s0, batch, seq, gla_bt)
    if ctx is None:
        o_na = _ctx_attn(nq, nk, nv, batch, seq)
    else:
        o_na = _na_attn(nq, nk, nv, ctx[0], ctx[1], ctx[4], batch, seq)
    x1, h2 = _merge(x2, mod, mod_row, of, ob, gr, o_na, sg, wts["g_gla"], wts["w_gla_o"], wts["w_na_o"],
                    wts["w_out"], wts["g_ffn"])
    y = _ffn(h2, x1, mod, mod_row, wts["w_up"], wts["w_conv"], wts["b_conv"], wts["w_down"], batch, seq,
             ffn_seqs)
    return y.reshape(batch, seq, D_MODEL), nk, nv, sf, sb


def _pack_weights(l, w_in, w_alpha_fwd, b_alpha_fwd, w_alpha_bwd, b_alpha_bwd, g_norm_mix, g_gla_norm,
                  g_q_norm, g_k_norm, w_gla_o, w_na_o, w_out, g_norm_ffn, w_up, w_conv, b_conv, w_down):
    w_pk = _pack_w_in(w_in[l])
    w_lr = jnp.zeros((LANES, 2 * GLA_QK), F32)
    w_lr = w_lr.at[:GLA_LOWRANK, :GLA_QK].set(w_alpha_fwd[l])
    w_lr = w_lr.at[GLA_LOWRANK:2 * GLA_LOWRANK, GLA_QK:].set(w_alpha_bwd[l]).astype(BF16)
    seg = jnp.asarray(np.kron(np.eye(NA_HEADS), np.ones((NA_HD, NA_HD))), BF16)
    return dict(
        g_mix=g_norm_mix[l][None], w_pk=w_pk, w_lr=w_lr,
        b_lr=jnp.concatenate([b_alpha_fwd[l], b_alpha_bwd[l]])[None],
        g_qn=jnp.tile(g_q_norm[l], NA_HEADS)[None], g_kn=jnp.tile(g_k_norm[l], NA_HEADS)[None], seg=seg,
        g_gla=jnp.tile(g_gla_norm[l], GLA_HEADS)[None],
        w_gla_o=w_gla_o[l].astype(BF16), w_na_o=w_na_o[l].astype(BF16), w_out=w_out[l].astype(BF16),
        g_ffn=g_norm_ffn[l][None], w_up=w_up[l].astype(BF16), w_conv=w_conv[l], b_conv=b_conv[l][None],
        w_down=w_down[l].astype(BF16))


def kernel(x_prompt, x_sample, c, cache_na_k, cache_na_v, state_gla_fwd, state_gla_bwd, c_ctx, w_ada, b_ada,
           g_norm_mix, w_in, w_alpha_fwd, b_alpha_fwd, w_alpha_bwd, b_alpha_bwd, g_gla_norm, g_q_norm, g_k_norm,
           rpb, w_gla_o, w_na_o, w_out, g_norm_ffn, w_up, w_conv, b_conv, w_down):
    depth = w_in.shape[0]
    batch, seq, _ = x_prompt.shape
    dec_batch, dec_seq, _ = x_sample.shape
    past = cache_na_k.shape[2]
    cond = jnp.concatenate([c_ctx[None], c], axis=0)
    xp, xs = x_prompt, x_sample
    new_k, new_v, new_sf, new_sb = [], [], [], []
    for l in range(depth):
        wts = _pack_weights(l, w_in, w_alpha_fwd, b_alpha_fwd, w_alpha_bwd, b_alpha_bwd, g_norm_mix, g_gla_norm,
                            g_q_norm, g_k_norm, w_gla_o, w_na_o, w_out, g_norm_ffn, w_up, w_conv, b_conv, w_down)
        mod = _ada(cond, w_ada[l], b_ada[l][None]).reshape(1 + dec_batch, 6, 1, D_MODEL)
        tbl = _bias_table(rpb[l])
        xp, kc, vc, sf, sb = _layer(xp, mod, lambda tok: 0, wts, None, gla_bt=4, ffn_seqs=2)
        new_k.append(kc.reshape(batch, seq, NA_HEADS, NA_HD))
        new_v.append(vc.reshape(batch, seq, NA_HEADS, NA_HD))
        new_sf.append(sf)
        new_sb.append(sb)
        ctx = (cache_na_k[:, l].reshape(dec_batch, past, NA_W), cache_na_v[:, l].reshape(dec_batch, past, NA_W),
               state_gla_fwd[:, l], state_gla_bwd[:, l], tbl)
        xs, _, _, _, _ = _layer(xs, mod, lambda tok: 1 + tok // dec_seq, wts, ctx, gla_bt=dec_batch,
                               ffn_seqs=1)
    return (xp, xs, jnp.stack(new_k, axis=1), jnp.stack(new_v, axis=1),
            jnp.stack(new_sf, axis=1), jnp.stack(new_sb, axis=1))
```

```python
import functools

import numpy as np
import jax
import jax.numpy as jnp
from jax import lax
from jax.experimental import pallas as pl
from jax.experimental.pallas import tpu as pltpu

F32 = jnp.float32
BF16 = jnp.bfloat16

D_MODEL = 1024
GRID_W = 64
GLA_HEADS, GLA_DK, GLA_DV = 4, 64, 128
GLA_QK = GLA_HEADS * GLA_DK
GLA_V = GLA_HEADS * GLA_DV
GLA_LOWRANK = 16
GLA_GATE_NORM = 16.0
GLA_CHUNK = 64
NA_HEADS, NA_HD = 8, 64
NA_W = NA_HEADS * NA_HD
NA_WIN_ROWS, NA_WIN_COLS = 8, 16
D_FF = 2816
EPS = 1e-6
IN_SIZES = (GLA_QK, GLA_QK, GLA_V, GLA_V, GLA_LOWRANK, GLA_LOWRANK, NA_W, NA_W, NA_W, D_MODEL, D_MODEL)

LANES = 128
ROW_TILE = 256
FF_TILE = 512
MASK_VALUE = -1e30
VMEM_LIMIT = 56 * 1024 * 1024

NT_DIMS = (((1,), (1,)), ((), ()))
TN_DIMS = (((0,), (0,)), ((), ()))


def _dot(a, b):
    return jnp.dot(a, b, preferred_element_type=F32)


def _dot_nt(a, b):
    return lax.dot_general(a, b, NT_DIMS, preferred_element_type=F32)


def _dot_tn(a, b):
    return lax.dot_general(a, b, TN_DIMS, preferred_element_type=F32)


def _split3(a):
    t1 = a.astype(BF16)
    e1 = a - t1.astype(F32)
    t2 = e1.astype(BF16)
    t3 = (e1 - t2.astype(F32)).astype(BF16)
    return t1, t2, t3


def _params(*sem):
    return pltpu.CompilerParams(dimension_semantics=sem, vmem_limit_bytes=VMEM_LIMIT)


def _const_spec(shape):
    zeros = (0,) * len(shape)
    return pl.BlockSpec(shape, lambda *_: zeros, pipeline_mode=pl.Buffered(1))


def _ada_kernel(c_ref, w_ref, b_ref, o_ref):
    c = c_ref[...]
    s = c * jax.nn.sigmoid(c)
    o_ref[...] = _dot(s.astype(BF16), w_ref[...].astype(BF16)) + b_ref[...]


def _ada(cond, w_ada, b_ada):
    n = cond.shape[0]
    return pl.pallas_call(
        _ada_kernel,
        grid=(6,),
        in_specs=[pl.BlockSpec((n, D_MODEL), lambda j: (0, 0)),
                  pl.BlockSpec((D_MODEL, D_MODEL), lambda j: (0, j)),
                  pl.BlockSpec((1, D_MODEL), lambda j: (0, j))],
        out_specs=pl.BlockSpec((n, D_MODEL), lambda j: (0, j)),
        out_shape=jax.ShapeDtypeStruct((n, 6 * D_MODEL), F32),
        compiler_params=_params("arbitrary"),
        name="ada",
    )(cond, w_ada, b_ada)


def _rms(x):
    return x * lax.rsqrt(jnp.mean(x * x, axis=-1, keepdims=True) + EPS)


_W_OFF = [int(o) for o in np.cumsum((0,) + IN_SIZES)]
W_GLA, W_LR, W_NA, W_MERGE, W_END = _W_OFF[0], _W_OFF[4], _W_OFF[6], _W_OFF[9], _W_OFF[11]
CAST_ROWS = 736


def _cast_kernel(w_ref, o_ref):
    o_ref[...] = w_ref[...].astype(BF16)


def _cast_bf16(w, rows):
    n, c = w.shape
    assert n % rows == 0
    return pl.pallas_call(
        _cast_kernel,
        grid=(n // rows,),
        in_specs=[pl.BlockSpec((rows, c), lambda i: (i, 0))],
        out_specs=pl.BlockSpec((rows, c), lambda i: (i, 0)),
        out_shape=jax.ShapeDtypeStruct((n, c), BF16),
        compiler_params=_params("parallel"),
        name="cast_bf16",
    )(w)


def _inproj_kernel(transpose_kv, x_ref, mod_ref, g_ref, w_ref, wlr_ref, blr_ref, gqn_ref, gkn_ref, seg_ref,
                   q_ref, k_ref, v_ref, gr_ref, la_ref, nq_ref, nk_ref, nv_ref, sg_ref):
    h = _rms(x_ref[...]) * g_ref[...]
    h = (h * (1.0 + mod_ref[1]) + mod_ref[0]).astype(BF16)

    u = _dot_nt(h, w_ref[W_GLA:W_LR, :])
    q_ref[...] = (u[:, :GLA_QK] * (GLA_DK ** -0.5)).astype(BF16)
    k_ref[...] = u[:, GLA_QK:2 * GLA_QK].astype(BF16)
    v_ref[...] = u[:, 2 * GLA_QK:2 * GLA_QK + GLA_V].astype(BF16)
    gr = u[:, 2 * GLA_QK + GLA_V:]
    gr_ref[...] = (gr * jax.nn.sigmoid(gr)).astype(BF16)

    z = _dot(_dot_nt(h, w_ref[W_LR:W_LR + LANES, :]).astype(BF16), wlr_ref[...]) + blr_ref[...]
    log_sig = jnp.minimum(z, 0.0) - jnp.log1p(jnp.exp(-jnp.abs(z)))
    la_ref[...] = log_sig * (1.0 / GLA_GATE_NORM)

    u = _dot_nt(h, w_ref[W_NA:W_MERGE, :])

    def head_norm(a, gain):
        ssq = _dot((a * a).astype(BF16), seg_ref[...])
        return a * lax.rsqrt(ssq * (1.0 / NA_HD) + EPS) * gain

    nq_ref[...] = (head_norm(u[:, :NA_W], gqn_ref[...]) * (NA_HD ** -0.5)).astype(BF16)
    nk = head_norm(u[:, NA_W:2 * NA_W], gkn_ref[...])
    nv = u[:, 2 * NA_W:]
    if transpose_kv:
        nk_ref[...] = nk.T
        nv_ref[...] = nv.T
    else:
        nk_ref[...] = nk.astype(nk_ref.dtype)
        nv_ref[...] = nv.astype(nv_ref.dtype)

    sg_ref[...] = jax.nn.sigmoid(_dot_nt(h, w_ref[W_MERGE:W_END, :])).astype(BF16)


def _inproj(x, mod, mod_row, g_mix, w_t, w_lr, b_lr, g_qn, g_kn, seg, batch, seq, transpose_kv):
    n = x.shape[0]
    tm = ROW_TILE
    assert seq % tm == 0 and W_LR % 16 == 0 and W_NA % 16 == 0 and W_MERGE % 16 == 0
    row = lambda c: pl.BlockSpec((tm, c), lambda i: (i, 0))
    outs = [(GLA_QK, BF16), (GLA_QK, BF16), (GLA_V, BF16), (GLA_V, BF16), (2 * GLA_QK, F32), (NA_W, BF16)]
    out_specs = [row(c) for c, _ in outs]
    out_shape = [jax.ShapeDtypeStruct((n, c), dt) for c, dt in outs]
    if transpose_kv:
        tps = seq // tm
        kv_spec = pl.BlockSpec((None, NA_W, tm), lambda i: (i // tps, 0, i % tps))
        kv_shape = jax.ShapeDtypeStruct((batch, NA_W, seq), F32)
    else:
        kv_spec = row(NA_W)
        kv_shape = jax.ShapeDtypeStruct((n, NA_W), BF16)
    out_specs += [kv_spec, kv_spec, row(2 * D_MODEL)]
    out_shape += [kv_shape, kv_shape, jax.ShapeDtypeStruct((n, 2 * D_MODEL), BF16)]
    return pl.pallas_call(
        functools.partial(_inproj_kernel, transpose_kv),
        grid=(n // tm,),
        in_specs=[row(D_MODEL),
                  pl.BlockSpec((None, 6, 1, D_MODEL), lambda i: (mod_row(i * tm), 0, 0, 0)),
                  _const_spec(g_mix.shape), _const_spec(w_t.shape), _const_spec(w_lr.shape),
                  _const_spec(b_lr.shape), _const_spec(g_qn.shape), _const_spec(g_kn.shape),
                  _const_spec(seg.shape)],
        out_specs=out_specs,
        out_shape=out_shape,
        compiler_params=_params("parallel"),
        name="inproj",
    )(x, mod, g_mix, w_t, w_lr, b_lr, g_qn, g_kn, seg)


def _gla_kernel(has_s0, nchunks, *refs):
    qf, kf, vf, laf, qb, kb, vb, lab = refs[:8]
    if has_s0:
        s0f, s0b, of, ob, sf, sb, st = refs[8:]
    else:
        of, ob, sf, sb, st = refs[8:]
        s0f = s0b = None
    n = pl.program_id(1)
    bt = qf.shape[0]
    C, H, DK, DV = GLA_CHUNK, GLA_HEADS, GLA_DK, GLA_DV
    row = lax.broadcasted_iota(jnp.int32, (C, C), 0)
    col = lax.broadcasted_iota(jnp.int32, (C, C), 1)

    @pl.when(n == 0)
    def _init():
        for d, s0 in enumerate((s0f, s0b)):
            for b in range(bt):
                for h in range(H):
                    st[d, b, h] = s0[b, h].T if has_s0 else jnp.zeros((DV, DK), F32)

    for d, (q_ref, k_ref, v_ref, la_ref, o_ref) in enumerate(((qf, kf, vf, laf, of), (qb, kb, vb, lab, ob))):
        tri = (row >= col) if d == 0 else (row <= col)
        tri_m = jnp.where(tri, 1.0, 0.0).astype(BF16)
        for b in range(bt):
            l1, l2, l3 = _split3(la_ref[b])
            bc = _dot(tri_m, l1) + _dot(tri_m, l2) + _dot(tri_m, l3)
            tot = bc[C - 1:C, :] if d == 0 else bc[0:1, :]
            q = q_ref[b].astype(F32)
            k = k_ref[b].astype(F32)
            v = v_ref[b]
            qt = (q * jnp.exp(bc)).astype(BF16)
            kt = (k * jnp.exp(-bc)).astype(BF16)
            kh = (k * jnp.exp(tot - bc)).astype(BF16)
            et = jnp.exp(tot)
            outs = []
            for h in range(H):
                ks = slice(h * DK, (h + 1) * DK)
                vs = slice(h * DV, (h + 1) * DV)
                s = jnp.where(tri, _dot_nt(qt[:, ks], kt[:, ks]), 0.0).astype(BF16)
                s_prev = st[d, b, h]
                outs.append(_dot(s, v[:, vs]) + _dot_nt(qt[:, ks], s_prev.astype(BF16)))
                st[d, b, h] = et[:, ks] * s_prev + _dot_tn(v[:, vs], kh[:, ks])
            o_ref[b] = jnp.concatenate(outs, axis=-1).astype(o_ref.dtype)

    @pl.when(n == nchunks - 1)
    def _fin():
        for b in range(bt):
            for h in range(H):
                sf[b, h] = st[0, b, h].T
                sb[b, h] = st[1, b, h].T


def _gla(q, k, v, la, s0, batch, seq, bt):
    C = GLA_CHUNK
    nc = seq // C
    q, k, v, la = (a.reshape(batch, seq, a.shape[-1]) for a in (q, k, v, la))
    fwd = lambda c, j=0: pl.BlockSpec((bt, C, c), lambda i, n: (i, n, j))
    bwd = lambda c, j=0: pl.BlockSpec((bt, C, c), lambda i, n: (i, nc - 1 - n, j))
    st_spec = pl.BlockSpec((bt, GLA_HEADS, GLA_DK, GLA_DV), lambda i, n: (i, 0, 0, 0))
    in_specs = [fwd(GLA_QK), fwd(GLA_QK), fwd(GLA_V), fwd(GLA_QK, 0),
                bwd(GLA_QK), bwd(GLA_QK), bwd(GLA_V), bwd(GLA_QK, 1)]
    args = [q, k, v, la, q, k, v, la]
    if s0 is not None:
        in_specs += [st_spec, st_spec]
        args += list(s0)
    st_shape = jax.ShapeDtypeStruct((batch, GLA_HEADS, GLA_DK, GLA_DV), F32)
    o_shape = jax.ShapeDtypeStruct((batch, seq, GLA_V), BF16)
    of, ob, sf, sb = pl.pallas_call(
        functools.partial(_gla_kernel, s0 is not None, nc),
        grid=(batch // bt, nc),
        in_specs=in_specs,
        out_specs=[fwd(GLA_V), bwd(GLA_V), st_spec, st_spec],
        out_shape=[o_shape, o_shape, st_shape, st_shape],
        scratch_shapes=[pltpu.VMEM((2, bt, GLA_HEADS, GLA_DV, GLA_DK), F32)],
        compiler_params=_params("parallel", "arbitrary"),
        name="gla",
    )(*args)
    return of.reshape(batch * seq, GLA_V), ob.reshape(batch * seq, GLA_V), sf, sb


def _softmax_pv(scores, values):
    m = functools.reduce(jnp.maximum, [jnp.max(s, axis=-1, keepdims=True) for s in scores])
    ps = [jnp.exp(s - m) for s in scores]
    l = functools.reduce(jnp.add, [jnp.sum(p, axis=-1, keepdims=True) for p in ps])
    o = functools.reduce(jnp.add, [(_dot_nt if transposed else _dot)(p.astype(BF16), v)
                                   for p, (v, transposed) in zip(ps, values)])
    return o / l


def _ctx_attn_kernel(q_ref, kt_ref, vt_ref, o_ref):
    q = q_ref[...]
    kt = kt_ref[...].astype(BF16)
    vt = vt_ref[...].astype(BF16)
    outs = []
    for h in range(NA_HEADS):
        hs = slice(h * NA_HD, (h + 1) * NA_HD)
        outs.append(_softmax_pv([_dot(q[:, hs], kt[hs, :])], [(vt[hs, :], True)]))
    o_ref[...] = jnp.concatenate(outs, axis=-1).astype(o_ref.dtype)


def _ctx_attn(q, kt, vt, batch, seq):
    spec = pl.BlockSpec((seq, NA_W), lambda b: (b, 0))
    spec_t = pl.BlockSpec((None, NA_W, seq), lambda b: (b, 0, 0))
    return pl.pallas_call(
        _ctx_attn_kernel,
        grid=(batch,),
        in_specs=[spec, spec_t, spec_t],
        out_specs=spec,
        out_shape=jax.ShapeDtypeStruct((batch * seq, NA_W), BF16),
        compiler_params=_params("parallel"),
        name="ctx_attn",
    )(q, kt, vt)


N_DR = 2 * NA_WIN_ROWS - 1
N_DC = 2 * NA_WIN_COLS - 1
DC_PAD = 32
N_PAIR = N_DR - 1


def _bias_kernel(r_ref, o_ref):
    qc = lax.broadcasted_iota(jnp.int32, (GRID_W, LANES), 0)
    kc = lax.broadcasted_iota(jnp.int32, (GRID_W, LANES), 1) % GRID_W
    start = jnp.clip(qc - NA_WIN_COLS // 2, 0, GRID_W - NA_WIN_COLS)
    in_win = (kc >= start) & (kc < start + NA_WIN_COLS)
    shift = LANES - (NA_WIN_COLS - 1)
    for p in range(o_ref.shape[0]):
        row = jnp.broadcast_to(r_ref[p:p + 1, :], (GRID_W, LANES))
        tile = pltpu.roll(row, shift, axis=1, stride=1, stride_axis=0)
        o_ref[p] = jnp.where(in_win, tile, MASK_VALUE)


def _bias_table(rpb):
    assert 2 * GRID_W == LANES
    pad = jnp.zeros((NA_HEADS, N_PAIR, GRID_W - N_DC), F32)
    r = jnp.concatenate([rpb[:, :N_PAIR], pad, rpb[:, 1:], pad], axis=-1).reshape(NA_HEADS * N_PAIR, LANES)
    t = pl.pallas_call(
        _bias_kernel,
        out_shape=jax.ShapeDtypeStruct((NA_HEADS * N_PAIR, GRID_W, LANES), F32),
        compiler_params=pltpu.CompilerParams(vmem_limit_bytes=VMEM_LIMIT),
        name="na_bias",
    )(r)
    return t.reshape(NA_HEADS, N_PAIR, GRID_W, LANES)


def _na_kernel(rows, q_ref, k_ref, v_ref, kct_ref, vct_ref, tbl_ref, o_ref, ctx_ref):
    r = pl.program_id(1)

    @pl.when(r == 0)
    def _cast_ctx():
        ctx_ref[0] = kct_ref[...].astype(BF16)
        ctx_ref[1] = vct_ref[...].astype(BF16)

    wr = min(NA_WIN_ROWS, rows)
    r0 = jnp.clip(r - wr // 2, 0, rows - wr)
    dr0 = r0 - r + NA_WIN_ROWS - 1
    start = pl.multiple_of(r0 * GRID_W, GRID_W)
    q = q_ref[...]
    kl = k_ref[pl.ds(start, wr * GRID_W), :]
    vl = v_ref[pl.ds(start, wr * GRID_W), :]
    outs = []
    for h in range(NA_HEADS):
        hs = slice(h * NA_HD, (h + 1) * NA_HD)
        bias = jnp.concatenate([tbl_ref[h, dr0 + 2 * j] for j in range(wr // 2)], axis=-1)
        s_loc = _dot_nt(q[:, hs], kl[:, hs]) + bias
        s_ctx = _dot(q[:, hs], ctx_ref[0, hs, :])
        outs.append(_softmax_pv([s_loc, s_ctx], [(vl[:, hs], False), (ctx_ref[1, hs, :], True)]))
    o_ref[...] = jnp.concatenate(outs, axis=-1).astype(o_ref.dtype)


def _na_attn(q, k, v, kt_ctx, vt_ctx, tbl, batch, seq):
    rows = seq // GRID_W
    assert rows >= NA_WIN_ROWS and NA_WIN_ROWS % 2 == 0
    past = kt_ctx.shape[2]
    q, k, v = (a.reshape(batch, seq, NA_W) for a in (q, k, v))
    whole = pl.BlockSpec((None, seq, NA_W), lambda b, r: (b, 0, 0))
    whole_t = pl.BlockSpec((None, NA_W, past), lambda b, r: (b, 0, 0))
    tile = pl.BlockSpec((None, GRID_W, NA_W), lambda b, r: (b, r, 0))
    o = pl.pallas_call(
        functools.partial(_na_kernel, rows),
        grid=(batch, rows),
        in_specs=[tile, whole, whole, whole_t, whole_t, _const_spec(tbl.shape)],
        out_specs=tile,
        out_shape=jax.ShapeDtypeStruct((batch, seq, NA_W), BF16),
        scratch_shapes=[pltpu.VMEM((2, NA_W, past), BF16)],
        compiler_params=_params("parallel", "arbitrary"),
        name="na_attn",
    )(q, k, v, kt_ctx, vt_ctx, tbl)
    return o.reshape(batch * seq, NA_W)


def _merge_kernel(x_ref, mod_ref, of_ref, ob_ref, gr_ref, na_ref, sg_ref, ggla_ref, wg_ref, wn_ref, wo_ref,
                  gffn_ref, x1_ref, h2_ref):
    o = of_ref[...].astype(F32) + ob_ref[...].astype(F32)
    on = jnp.concatenate([_rms(o[:, h * GLA_DV:(h + 1) * GLA_DV]) for h in range(GLA_HEADS)], axis=-1)
    on = on * ggla_ref[...] * gr_ref[...].astype(F32)
    b_gla = _dot(on.astype(BF16), wg_ref[...])
    b_na = _dot(na_ref[...], wn_ref[...])
    sg = sg_ref[...].astype(F32)
    mix = _dot((sg[:, :D_MODEL] * b_gla + sg[:, D_MODEL:] * b_na).astype(BF16), wo_ref[...])
    x1 = x_ref[...] + mod_ref[2] * mix
    x1_ref[...] = x1
    h2_ref[...] = (_rms(x1) * gffn_ref[...] * (1.0 + mod_ref[4]) + mod_ref[3]).astype(BF16)


def _merge(x, mod, mod_row, of, ob, gr, na, sg, g_gla, w_gla_o, w_na_o, w_out, g_ffn):
    n = x.shape[0]
    tm = ROW_TILE
    row = lambda c: pl.BlockSpec((tm, c), lambda i: (i, 0))
    return pl.pallas_call(
        _merge_kernel,
        grid=(n // tm,),
        in_specs=[row(D_MODEL),
                  pl.BlockSpec((None, 6, 1, D_MODEL), lambda i: (mod_row(i * tm), 0, 0, 0)),
                  row(GLA_V), row(GLA_V), row(GLA_V), row(NA_W), row(2 * D_MODEL),
                  _const_spec(g_gla.shape), _const_spec(w_gla_o.shape), _const_spec(w_na_o.shape),
                  _const_spec(w_out.shape), _const_spec(g_ffn.shape)],
        out_specs=[row(D_MODEL), row(D_MODEL)],
        out_shape=[jax.ShapeDtypeStruct((n, D_MODEL), F32), jax.ShapeDtypeStruct((n, D_MODEL), BF16)],
        compiler_params=_params("parallel"),
        name="merge",
    )(x, mod, of, ob, gr, na, sg, g_gla, w_gla_o, w_na_o, w_out, g_ffn)


def _ffn_kernel(seq, h_ref, x1_ref, mod_ref, wup_ref, wc_ref, bc_ref, wdn_ref, o_ref, a_ref, g_ref):
    t = h_ref.shape[0]
    h = h_ref[...]
    pos = lax.broadcasted_iota(jnp.int32, (t, 1), 0) % seq
    first = pos == 0
    last = pos == seq - 1
    tiles = [(lo, min(lo + FF_TILE, D_FF)) for lo in range(0, D_FF, FF_TILE)]

    def up(j):
        lo, hi = tiles[j]
        a_ref[j % 2, :, :hi - lo] = _dot(h, wup_ref[:, lo:hi])
        g_ref[j % 2, :, :hi - lo] = _dot(h, wup_ref[:, D_FF + lo:D_FF + hi])

    def conv(u, lo, hi):
        prev = jnp.where(first, 0.0, pltpu.roll(u, 1, axis=0))
        nxt = jnp.where(last, 0.0, pltpu.roll(u, t - 1, axis=0))
        return wc_ref[0:1, lo:hi] * prev + wc_ref[1:2, lo:hi] * u + wc_ref[2:3, lo:hi] * nxt + bc_ref[:, lo:hi]

    up(0)
    for j, (lo, hi) in enumerate(tiles):
        if j + 1 < len(tiles):
            up(j + 1)
        a = conv(a_ref[j % 2, :, :hi - lo], lo, hi)
        g = conv(g_ref[j % 2, :, :hi - lo], D_FF + lo, D_FF + hi)
        part = _dot((a * (g * jax.nn.sigmoid(g))).astype(BF16), wdn_ref[lo:hi, :])
        if j == 0:
            o_ref[...] = part
        else:
            o_ref[...] += part
    o_ref[...] = x1_ref[...] + mod_ref[5] * o_ref[...]


def _ffn(h2, x1, mod, mod_row, w_up, w_conv, b_conv, w_down, batch, seq, seqs_per_step):
    rows = seq * seqs_per_step
    blk = lambda: pl.BlockSpec((rows, D_MODEL), lambda b: (b, 0))
    return pl.pallas_call(
        functools.partial(_ffn_kernel, seq),
        grid=(batch // seqs_per_step,),
        in_specs=[blk(), blk(),
                  pl.BlockSpec((None, 6, 1, D_MODEL), lambda b: (mod_row(b * rows), 0, 0, 0)),
                  _const_spec(w_up.shape), _const_spec(w_conv.shape), _const_spec(b_conv.shape),
                  _const_spec(w_down.shape)],
        out_specs=blk(),
        out_shape=jax.ShapeDtypeStruct((batch * seq, D_MODEL), F32),
        scratch_shapes=[pltpu.VMEM((2, rows, FF_TILE), F32), pltpu.VMEM((2, rows, FF_TILE), F32)],
        compiler_params=_params("parallel"),
        name="conv_ffn",
    )(h2, x1, mod, w_up, w_conv, b_conv, w_down)


def _layer(x, mod, mod_row, wts, ctx, gla_bt, ffn_seqs):
    batch, seq, _ = x.shape
    x2 = x.reshape(batch * seq, D_MODEL)
    q, k, v, gr, la, nq, nk, nv, sg = _inproj(x2, mod, mod_row, wts["g_mix"], wts["w_t"], wts["w_lr"],
                                              wts["b_lr"], wts["g_qn"], wts["g_kn"], wts["seg"], batch, seq,
                                              transpose_kv=ctx is None)
    s0 = None if ctx is None else (ctx[2], ctx[3])
    of, ob, sf, sb = _gla(q, k, v, la, s0, batch, seq, gla_bt)
    if ctx is None:
        o_na = _ctx_attn(nq, nk, nv, batch, seq)
    else:
        o_na = _na_attn(nq, nk, nv, ctx[0], ctx[1], ctx[4], batch, seq)
    x1, h2 = _merge(x2, mod, mod_row, of, ob, gr, o_na, sg, wts["g_gla"], wts["w_gla_o"], wts["w_na_o"],
                    wts["w_out"], wts["g_ffn"])
    y = _ffn(h2, x1, mod, mod_row, wts["w_up"], wts["w_conv"], wts["b_conv"], wts["w_down"], batch, seq,
             ffn_seqs)
    return y.reshape(batch, seq, D_MODEL), nk, nv, sf, sb


def _pack_weights(l, w_in, w_alpha_fwd, b_alpha_fwd, w_alpha_bwd, b_alpha_bwd, g_norm_mix, g_gla_norm,
                  g_q_norm, g_k_norm, w_gla_o, w_na_o, w_out, g_norm_ffn, w_up, w_conv, b_conv, w_down):
    w_t = _cast_bf16(jnp.swapaxes(w_in[l], 0, 1), CAST_ROWS)
    w_lr = jnp.zeros((LANES, 2 * GLA_QK), F32)
    w_lr = w_lr.at[:GLA_LOWRANK, :GLA_QK].set(w_alpha_fwd[l])
    w_lr = w_lr.at[GLA_LOWRANK:2 * GLA_LOWRANK, GLA_QK:].set(w_alpha_bwd[l]).astype(BF16)
    seg = jnp.asarray(np.kron(np.eye(NA_HEADS), np.ones((NA_HD, NA_HD))), BF16)
    return dict(
        g_mix=g_norm_mix[l][None], w_t=w_t, w_lr=w_lr,
        b_lr=jnp.concatenate([b_alpha_fwd[l], b_alpha_bwd[l]])[None],
        g_qn=jnp.tile(g_q_norm[l], NA_HEADS)[None], g_kn=jnp.tile(g_k_norm[l], NA_HEADS)[None], seg=seg,
        g_gla=jnp.tile(g_gla_norm[l], GLA_HEADS)[None],
        w_gla_o=w_gla_o[l].astype(BF16), w_na_o=w_na_o[l].astype(BF16), w_out=w_out[l].astype(BF16),
        g_ffn=g_norm_ffn[l][None], w_up=w_up[l].astype(BF16), w_conv=w_conv[l], b_conv=b_conv[l][None],
        w_down=w_down[l].astype(BF16))


def kernel(x_prompt, x_sample, c, cache_na_k, cache_na_v, state_gla_fwd, state_gla_bwd, c_ctx, w_ada, b_ada,
           g_norm_mix, w_in, w_alpha_fwd, b_alpha_fwd, w_alpha_bwd, b_alpha_bwd, g_gla_norm, g_q_norm, g_k_norm,
           rpb, w_gla_o, w_na_o, w_out, g_norm_ffn, w_up, w_conv, b_conv, w_down):
    depth = w_in.shape[0]
    batch, seq, _ = x_prompt.shape
    dec_batch, dec_seq, _ = x_sample.shape
    past = cache_na_k.shape[2]
    cond = jnp.concatenate([c_ctx[None], c], axis=0)
    xp, xs = x_prompt, x_sample
    new_k, new_v, new_sf, new_sb = [], [], [], []
    for l in range(depth):
        wts = _pack_weights(l, w_in, w_alpha_fwd, b_alpha_fwd, w_alpha_bwd, b_alpha_bwd, g_norm_mix, g_gla_norm,
                            g_q_norm, g_k_norm, w_gla_o, w_na_o, w_out, g_norm_ffn, w_up, w_conv, b_conv, w_down)
        mod = _ada(cond, w_ada[l], b_ada[l][None]).reshape(1 + dec_batch, 6, 1, D_MODEL)
        tbl = _bias_table(rpb[l])
        xp, kc, vc, sf, sb = _layer(xp, mod, lambda tok: 0, wts, None, gla_bt=4, ffn_seqs=2)
        to_cache = lambda a: a.reshape(batch, NA_HEADS, NA_HD, seq).transpose(0, 3, 1, 2)
        from_cache = lambda a: a.transpose(0, 2, 3, 1).reshape(dec_batch, NA_W, past)
        new_k.append(to_cache(kc))
        new_v.append(to_cache(vc))
        new_sf.append(sf)
        new_sb.append(sb)
        ctx = (from_cache(cache_na_k[:, l]), from_cache(cache_na_v[:, l]),
               state_gla_fwd[:, l], state_gla_bwd[:, l], tbl)
        xs, _, _, _, _ = _layer(xs, mod, lambda tok: 1 + tok // dec_seq, wts, ctx, gla_bt=dec_batch,
                               ffn_seqs=1)
    return (xp, xs, jnp.stack(new_k, axis=1), jnp.stack(new_v, axis=1),
            jnp.stack(new_sf, axis=1), jnp.stack(new_sb, axis=1))
```

```python
import functools

import numpy as np
import jax
import jax.numpy as jnp
from jax import lax
from jax.experimental import pallas as pl
from jax.experimental.pallas import tpu as pltpu

F32 = jnp.float32
BF16 = jnp.bfloat16

D_MODEL = 1024
GRID_W = 64
GLA_HEADS, GLA_DK, GLA_DV = 4, 64, 128
GLA_QK = GLA_HEADS * GLA_DK
GLA_V = GLA_HEADS * GLA_DV
GLA_LOWRANK = 16
GLA_GATE_NORM = 16.0
GLA_CHUNK = 64
NA_HEADS, NA_HD = 8, 64
NA_W = NA_HEADS * NA_HD
NA_WIN_ROWS, NA_WIN_COLS = 8, 16
D_FF = 2816
EPS = 1e-6
IN_SIZES = (GLA_QK, GLA_QK, GLA_V, GLA_V, GLA_LOWRANK, GLA_LOWRANK, NA_W, NA_W, NA_W, D_MODEL, D_MODEL)

LANES = 128
ROW_TILE = 256
FF_TILE = 512
MASK_VALUE = -1e30
VMEM_LIMIT = 56 * 1024 * 1024

NT_DIMS = (((1,), (1,)), ((), ()))
TN_DIMS = (((0,), (0,)), ((), ()))


def _dot(a, b):
    return jnp.dot(a, b, preferred_element_type=F32)


def _dot_nt(a, b):
    return lax.dot_general(a, b, NT_DIMS, preferred_element_type=F32)


def _dot_tn(a, b):
    return lax.dot_general(a, b, TN_DIMS, preferred_element_type=F32)


def _split3(a):
    t1 = a.astype(BF16)
    e1 = a - t1.astype(F32)
    t2 = e1.astype(BF16)
    t3 = (e1 - t2.astype(F32)).astype(BF16)
    return t1, t2, t3


def _params(*sem):
    return pltpu.CompilerParams(dimension_semantics=sem, vmem_limit_bytes=VMEM_LIMIT)


def _const_spec(shape):
    zeros = (0,) * len(shape)
    return pl.BlockSpec(shape, lambda *_: zeros, pipeline_mode=pl.Buffered(1))


def _ada_kernel(c_ref, w_ref, b_ref, o_ref):
    c = c_ref[...]
    s = c * jax.nn.sigmoid(c)
    o_ref[...] = _dot(s.astype(BF16), w_ref[...].astype(BF16)) + b_ref[...]


def _ada(cond, w_ada, b_ada):
    n = cond.shape[0]
    return pl.pallas_call(
        _ada_kernel,
        grid=(6,),
        in_specs=[pl.BlockSpec((n, D_MODEL), lambda j: (0, 0)),
                  pl.BlockSpec((D_MODEL, D_MODEL), lambda j: (0, j)),
                  pl.BlockSpec((1, D_MODEL), lambda j: (0, j))],
        out_specs=pl.BlockSpec((n, D_MODEL), lambda j: (0, j)),
        out_shape=jax.ShapeDtypeStruct((n, 6 * D_MODEL), F32),
        compiler_params=_params("arbitrary"),
        name="ada",
    )(cond, w_ada, b_ada)


def _rms(x):
    return x * lax.rsqrt(jnp.mean(x * x, axis=-1, keepdims=True) + EPS)


_W_OFF = [int(o) for o in np.cumsum((0,) + IN_SIZES)]
W_GLA, W_LR, W_NA, W_MERGE, W_END = _W_OFF[0], _W_OFF[4], _W_OFF[6], _W_OFF[9], _W_OFF[11]
CAST_ROWS = 736


def _cast_kernel(w_ref, o_ref):
    o_ref[...] = w_ref[...].astype(BF16)


def _cast_bf16(w, rows):
    n, c = w.shape
    assert n % rows == 0
    return pl.pallas_call(
        _cast_kernel,
        grid=(n // rows,),
        in_specs=[pl.BlockSpec((rows, c), lambda i: (i, 0))],
        out_specs=pl.BlockSpec((rows, c), lambda i: (i, 0)),
        out_shape=jax.ShapeDtypeStruct((n, c), BF16),
        compiler_params=_params("parallel"),
        name="cast_bf16",
    )(w)


def _inproj_kernel(transpose_kv, x_ref, mod_ref, g_ref, w_ref, wlr_ref, blr_ref, gqn_ref, gkn_ref, seg_ref,
                   q_ref, k_ref, v_ref, gr_ref, la_ref, nq_ref, nk_ref, nv_ref, sg_ref):
    h = _rms(x_ref[...]) * g_ref[...]
    h = (h * (1.0 + mod_ref[1]) + mod_ref[0]).astype(BF16)

    u = _dot_nt(h, w_ref[W_GLA:W_LR, :])
    q_ref[...] = (u[:, :GLA_QK] * (GLA_DK ** -0.5)).astype(BF16)
    k_ref[...] = u[:, GLA_QK:2 * GLA_QK].astype(BF16)
    v_ref[...] = u[:, 2 * GLA_QK:2 * GLA_QK + GLA_V].astype(BF16)
    gr = u[:, 2 * GLA_QK + GLA_V:]
    gr_ref[...] = (gr * jax.nn.sigmoid(gr)).astype(BF16)

    z = _dot(_dot_nt(h, w_ref[W_LR:W_LR + LANES, :]).astype(BF16), wlr_ref[...]) + blr_ref[...]
    log_sig = jnp.minimum(z, 0.0) - jnp.log1p(jnp.exp(-jnp.abs(z)))
    la_ref[...] = log_sig * (1.0 / GLA_GATE_NORM)

    u = _dot_nt(h, w_ref[W_NA:W_MERGE, :])

    def head_norm(a, gain):
        ssq = _dot((a * a).astype(BF16), seg_ref[...])
        return a * lax.rsqrt(ssq * (1.0 / NA_HD) + EPS) * gain

    nq_ref[...] = (head_norm(u[:, :NA_W], gqn_ref[...]) * (NA_HD ** -0.5)).astype(BF16)
    nk = head_norm(u[:, NA_W:2 * NA_W], gkn_ref[...])
    nv = u[:, 2 * NA_W:]
    if transpose_kv:
        nk_ref[...] = nk.T
        nv_ref[...] = nv.T
    else:
        nk_ref[...] = nk.astype(nk_ref.dtype)
        nv_ref[...] = nv.astype(nv_ref.dtype)

    sg_ref[...] = jax.nn.sigmoid(_dot_nt(h, w_ref[W_MERGE:W_END, :])).astype(BF16)


def _inproj(x, mod, mod_row, g_mix, w_t, w_lr, b_lr, g_qn, g_kn, seg, batch, seq, transpose_kv):
    n = x.shape[0]
    tm = ROW_TILE
    assert seq % tm == 0 and W_LR % 16 == 0 and W_NA % 16 == 0 and W_MERGE % 16 == 0
    row = lambda c: pl.BlockSpec((tm, c), lambda i: (i, 0))
    outs = [(GLA_QK, BF16), (GLA_QK, BF16), (GLA_V, BF16), (GLA_V, BF16), (2 * GLA_QK, F32), (NA_W, BF16)]
    out_specs = [row(c) for c, _ in outs]
    out_shape = [jax.ShapeDtypeStruct((n, c), dt) for c, dt in outs]
    if transpose_kv:
        tps = seq // tm
        kv_spec = pl.BlockSpec((None, NA_W, tm), lambda i: (i // tps, 0, i % tps))
        kv_shape = jax.ShapeDtypeStruct((batch, NA_W, seq), F32)
    else:
        kv_spec = row(NA_W)
        kv_shape = jax.ShapeDtypeStruct((n, NA_W), BF16)
    out_specs += [kv_spec, kv_spec, row(2 * D_MODEL)]
    out_shape += [kv_shape, kv_shape, jax.ShapeDtypeStruct((n, 2 * D_MODEL), BF16)]
    return pl.pallas_call(
        functools.partial(_inproj_kernel, transpose_kv),
        grid=(n // tm,),
        in_specs=[row(D_MODEL),
                  pl.BlockSpec((None, 6, 1, D_MODEL), lambda i: (mod_row(i * tm), 0, 0, 0)),
                  _const_spec(g_mix.shape), _const_spec(w_t.shape), _const_spec(w_lr.shape),
                  _const_spec(b_lr.shape), _const_spec(g_qn.shape), _const_spec(g_kn.shape),
                  _const_spec(seg.shape)],
        out_specs=out_specs,
        out_shape=out_shape,
        compiler_params=_params("parallel"),
        name="inproj",
    )(x, mod, g_mix, w_t, w_lr, b_lr, g_qn, g_kn, seg)


GLA_SAFE_DECAY = 80.0


def _chunk_cumsum(x, reverse):
    c = x.shape[0]
    rows = lax.broadcasted_iota(jnp.int32, (c, 1), 0)
    s = 1
    while s < c:
        if reverse:
            x = x + jnp.where(rows < c - s, pltpu.roll(x, c - s, axis=0), 0.0)
        else:
            x = x + jnp.where(rows >= s, pltpu.roll(x, s, axis=0), 0.0)
        s *= 2
    return x


def _gla_kernel(has_s0, nchunks, *refs):
    qf, kf, vf, laf, qb, kb, vb, lab = refs[:8]
    if has_s0:
        s0f, s0b, of, ob, sf, sb, st, oi, q32, bcs = refs[8:]
    else:
        of, ob, sf, sb, st, oi, q32, bcs = refs[8:]
        s0f = s0b = None
    n = pl.program_id(1)
    bt = qf.shape[0]
    C, H, DK, DV = GLA_CHUNK, GLA_HEADS, GLA_DK, GLA_DV
    row = lax.broadcasted_iota(jnp.int32, (C, C), 0)
    col = lax.broadcasted_iota(jnp.int32, (C, C), 1)
    tri = ((row >= col), (row <= col))
    ins = ((qf, kf, vf, laf, of), (qb, kb, vb, lab, ob))
    units = [(d, b) for d in range(2) for b in range(bt)]
    heads = [(slice(h * DK, (h + 1) * DK), slice(h * DV, (h + 1) * DV)) for h in range(H)]

    @pl.when(n == 0)
    def _init():
        for d, s0 in enumerate((s0f, s0b)):
            for b in range(bt):
                for h in range(H):
                    st[d, b, h] = s0[b, h].T if has_s0 else jnp.zeros((DV, DK), F32)

    bc, tot, qt, kt, kh, et = {}, {}, {}, {}, {}, {}
    for u in units:
        d, b = u
        bc[u] = _chunk_cumsum(ins[d][3][b], reverse=d == 1)
        tot[u] = bc[u][C - 1:C, :] if d == 0 else bc[u][0:1, :]
    worst = functools.reduce(jnp.maximum, [jnp.max(-tot[u]) for u in units])
    safe = worst <= GLA_SAFE_DECAY

    for u in units:
        d, b = u
        q = ins[d][0][b].astype(F32)
        k = ins[d][1][b].astype(F32)
        qt[u] = (q * jnp.exp(bc[u])).astype(BF16)
        kh[u] = (k * jnp.exp(tot[u] - bc[u])).astype(BF16)
        et[u] = jnp.exp(tot[u])

    @pl.when(safe)
    def _intra_factorised():
        for u in units:
            d, b = u
            k = ins[d][1][b].astype(F32)
            kt[u] = (k * jnp.exp(-bc[u])).astype(BF16)
        sc = {}
        for u in units:
            for h, (ks, _) in enumerate(heads):
                sc[u, h] = jnp.where(tri[u[0]], _dot_nt(qt[u][:, ks], kt[u][:, ks]), 0.0).astype(BF16)
        for u in units:
            d, b = u
            v = ins[d][2][b]
            for h, (_, vs) in enumerate(heads):
                oi[d, b, h] = _dot(sc[u, h], v[:, vs])

    @pl.when(jnp.logical_not(safe))
    def _intra_pairwise():
        for u in units:
            d, b = u
            q32[d, b] = ins[d][0][b].astype(F32)
            bcs[d, b] = bc[u]
        svis = lax.broadcasted_iota(jnp.int32, (C, 1), 0)
        for u in units:
            d, b = u
            k = ins[d][1][b].astype(F32)
            v = ins[d][2][b].astype(F32)
            for h, (ks, vs) in enumerate(heads):
                kh_, bh, vh = k[:, ks], bc[u][:, ks], v[:, vs]

                def one_row(t, carry, d=d, b=b, h=h, ks=ks, kh_=kh_, bh=bh, vh=vh):
                    bt_row = bcs[d, b, pl.ds(t, 1), :][:, ks]
                    q_row = q32[d, b, pl.ds(t, 1), :][:, ks]
                    vis = (svis <= t) if d == 0 else (svis >= t)
                    w = jnp.exp(jnp.where(vis, bt_row - bh, -jnp.inf)) * kh_ * q_row
                    s_col = jnp.sum(w, axis=1, keepdims=True)
                    oi[d, b, h, pl.ds(t, 1), :] = jnp.sum(s_col * vh, axis=0, keepdims=True)
                    return carry

                lax.fori_loop(0, C, one_row, 0)

    for u in units:
        d, b = u
        outs = [oi[d, b, h] + _dot_nt(qt[u][:, ks], st[d, b, h].astype(BF16)) for h, (ks, _) in enumerate(heads)]
        ins[d][4][b] = jnp.concatenate(outs, axis=-1).astype(ins[d][4].dtype)
    for u in units:
        d, b = u
        v = ins[d][2][b]
        for h, (ks, vs) in enumerate(heads):
            st[d, b, h] = et[u][:, ks] * st[d, b, h] + _dot_tn(v[:, vs], kh[u][:, ks])

    @pl.when(n == nchunks - 1)
    def _fin():
        for b in range(bt):
            for h in range(H):
                sf[b, h] = st[0, b, h].T
                sb[b, h] = st[1, b, h].T


def _gla(q, k, v, la, s0, batch, seq, bt):
    C = GLA_CHUNK
    nc = seq // C
    q, k, v, la = (a.reshape(batch, seq, a.shape[-1]) for a in (q, k, v, la))
    fwd = lambda c, j=0: pl.BlockSpec((bt, C, c), lambda i, n: (i, n, j))
    bwd = lambda c, j=0: pl.BlockSpec((bt, C, c), lambda i, n: (i, nc - 1 - n, j))
    st_spec = pl.BlockSpec((bt, GLA_HEADS, GLA_DK, GLA_DV), lambda i, n: (i, 0, 0, 0))
    in_specs = [fwd(GLA_QK), fwd(GLA_QK), fwd(GLA_V), fwd(GLA_QK, 0),
                bwd(GLA_QK), bwd(GLA_QK), bwd(GLA_V), bwd(GLA_QK, 1)]
    args = [q, k, v, la, q, k, v, la]
    if s0 is not None:
        in_specs += [st_spec, st_spec]
        args += list(s0)
    st_shape = jax.ShapeDtypeStruct((batch, GLA_HEADS, GLA_DK, GLA_DV), F32)
    o_shape = jax.ShapeDtypeStruct((batch, seq, GLA_V), BF16)
    of, ob, sf, sb = pl.pallas_call(
        functools.partial(_gla_kernel, s0 is not None, nc),
        grid=(batch // bt, nc),
        in_specs=in_specs,
        out_specs=[fwd(GLA_V), bwd(GLA_V), st_spec, st_spec],
        out_shape=[o_shape, o_shape, st_shape, st_shape],
        scratch_shapes=[pltpu.VMEM((2, bt, GLA_HEADS, GLA_DV, GLA_DK), F32),
                        pltpu.VMEM((2, bt, GLA_HEADS, C, GLA_DV), F32),
                        pltpu.VMEM((2, bt, C, GLA_QK), F32),
                        pltpu.VMEM((2, bt, C, GLA_QK), F32)],
        compiler_params=_params("parallel", "arbitrary"),
        name="gla",
    )(*args)
    return of.reshape(batch * seq, GLA_V), ob.reshape(batch * seq, GLA_V), sf, sb


def _softmax_pv(scores, values):
    m = functools.reduce(jnp.maximum, [jnp.max(s, axis=-1, keepdims=True) for s in scores])
    ps = [jnp.exp(s - m) for s in scores]
    l = functools.reduce(jnp.add, [jnp.sum(p, axis=-1, keepdims=True) for p in ps])
    o = functools.reduce(jnp.add, [(_dot_nt if transposed else _dot)(p.astype(BF16), v)
                                   for p, (v, transposed) in zip(ps, values)])
    return o / l


def _ctx_attn_kernel(q_ref, kt_ref, vt_ref, o_ref):
    q = q_ref[...]
    kt = kt_ref[...].astype(BF16)
    vt = vt_ref[...].astype(BF16)
    outs = []
    for h in range(NA_HEADS):
        hs = slice(h * NA_HD, (h + 1) * NA_HD)
        outs.append(_softmax_pv([_dot(q[:, hs], kt[hs, :])], [(vt[hs, :], True)]))
    o_ref[...] = jnp.concatenate(outs, axis=-1).astype(o_ref.dtype)


def _ctx_attn(q, kt, vt, batch, seq):
    spec = pl.BlockSpec((seq, NA_W), lambda b: (b, 0))
    spec_t = pl.BlockSpec((None, NA_W, seq), lambda b: (b, 0, 0))
    return pl.pallas_call(
        _ctx_attn_kernel,
        grid=(batch,),
        in_specs=[spec, spec_t, spec_t],
        out_specs=spec,
        out_shape=jax.ShapeDtypeStruct((batch * seq, NA_W), BF16),
        compiler_params=_params("parallel"),
        name="ctx_attn",
    )(q, kt, vt)


N_DR = 2 * NA_WIN_ROWS - 1
N_DC = 2 * NA_WIN_COLS - 1
NA_QROWS = 4
NA_KROWS = NA_WIN_ROWS + NA_QROWS
BIAS_BOTH = 0
BIAS_RIGHT = N_DR - 1
BIAS_LEFT = BIAS_RIGHT + NA_WIN_ROWS
BIAS_NONE = BIAS_LEFT + NA_WIN_ROWS
BIAS_ROWS = 32


def _bias_rows(rpb):
    assert 2 * GRID_W == LANES and BIAS_NONE < BIAS_ROWS
    b = jnp.pad(rpb, ((0, 0), (0, 0), (0, GRID_W - N_DC)))
    m = jnp.full((NA_HEADS, NA_WIN_ROWS, GRID_W), MASK_VALUE, F32)
    both = jnp.concatenate([b[:, :-1], b[:, 1:]], axis=-1)
    right = jnp.concatenate([m, b[:, :NA_WIN_ROWS]], axis=-1)
    left = jnp.concatenate([b[:, NA_WIN_ROWS - 1:], m], axis=-1)
    none = jnp.concatenate([m[:, :1], m[:, :1]], axis=-1)
    pad = jnp.zeros((NA_HEADS, BIAS_ROWS - BIAS_NONE - 1, LANES), F32)
    return jnp.concatenate([both, right, left, none, pad], axis=1).reshape(NA_HEADS * BIAS_ROWS, LANES)


def _na_kernel(rows, q_ref, k_ref, v_ref, kct_ref, vct_ref, bias_ref, o_ref, ctx_ref):
    g = pl.program_id(1)

    @pl.when(g == 0)
    def _cast_ctx():
        ctx_ref[0] = kct_ref[...].astype(BF16)
        ctx_ref[1] = vct_ref[...].astype(BF16)

    wr = NA_WIN_ROWS
    ws = jnp.clip(g * NA_QROWS - wr // 2, 0, rows - NA_KROWS)
    start = pl.multiple_of(ws * GRID_W, GRID_W)
    q = q_ref[...]
    kl = k_ref[pl.ds(start, NA_KROWS * GRID_W), :]
    vl = v_ref[pl.ds(start, NA_KROWS * GRID_W), :]

    tile_row = []
    for i in range(NA_QROWS):
        r = g * NA_QROWS + i
        r0 = jnp.clip(r - wr // 2, 0, rows - wr)
        for j in range(NA_KROWS // 2):
            kr = ws + 2 * j
            in0 = (kr >= r0) & (kr < r0 + wr)
            in1 = (kr + 1 >= r0) & (kr + 1 < r0 + wr)
            dr = kr - r + wr - 1
            tile_row.append(jnp.where(in0 & in1, BIAS_BOTH + dr,
                                      jnp.where(in1, BIAS_RIGHT + dr + 1,
                                                jnp.where(in0, BIAS_LEFT + dr - (wr - 1), BIAS_NONE))))

    qc = lax.broadcasted_iota(jnp.int32, (GRID_W, LANES), 0)
    kc = lax.broadcasted_iota(jnp.int32, (GRID_W, LANES), 1) % GRID_W
    c0 = jnp.clip(qc - NA_WIN_COLS // 2, 0, GRID_W - NA_WIN_COLS)
    in_win = (kc >= c0) & (kc < c0 + NA_WIN_COLS)
    shift = LANES - (NA_WIN_COLS - 1)

    def bias_tile(h, t):
        row = jnp.broadcast_to(bias_ref[pl.ds(h * BIAS_ROWS + t, 1), :], (GRID_W, LANES))
        return jnp.where(in_win, pltpu.roll(row, shift, axis=1, stride=1, stride_axis=0), MASK_VALUE)

    outs = []
    for h in range(NA_HEADS):
        hs = slice(h * NA_HD, (h + 1) * NA_HD)
        bias = jnp.concatenate(
            [jnp.concatenate([bias_tile(h, tile_row[i * (NA_KROWS // 2) + j]) for j in range(NA_KROWS // 2)], axis=1)
             for i in range(NA_QROWS)], axis=0)
        s_loc = _dot_nt(q[:, hs], kl[:, hs]) + bias
        s_ctx = _dot(q[:, hs], ctx_ref[0, hs, :])
        outs.append(_softmax_pv([s_loc, s_ctx], [(vl[:, hs], False), (ctx_ref[1, hs, :], True)]))
    o_ref[...] = jnp.concatenate(outs, axis=-1).astype(o_ref.dtype)


def _na_attn(q, k, v, kt_ctx, vt_ctx, bias_rows, batch, seq):
    rows = seq // GRID_W
    wr = NA_WIN_ROWS
    assert rows >= NA_KROWS and rows % NA_QROWS == 0 and wr % 2 == 0 and NA_KROWS % 2 == 0
    for g in range(rows // NA_QROWS):
        ws = min(max(g * NA_QROWS - wr // 2, 0), rows - NA_KROWS)
        for r in range(g * NA_QROWS, (g + 1) * NA_QROWS):
            r0 = min(max(r - wr // 2, 0), rows - wr)
            assert ws <= r0 and r0 + wr <= ws + NA_KROWS
    past = kt_ctx.shape[2]
    q, k, v = (a.reshape(batch, seq, NA_W) for a in (q, k, v))
    whole = pl.BlockSpec((None, seq, NA_W), lambda b, g: (b, 0, 0))
    whole_t = pl.BlockSpec((None, NA_W, past), lambda b, g: (b, 0, 0))
    tile = pl.BlockSpec((None, NA_QROWS * GRID_W, NA_W), lambda b, g: (b, g, 0))
    o = pl.pallas_call(
        functools.partial(_na_kernel, rows),
        grid=(batch, rows // NA_QROWS),
        in_specs=[tile, whole, whole, whole_t, whole_t, _const_spec(bias_rows.shape)],
        out_specs=tile,
        out_shape=jax.ShapeDtypeStruct((batch, seq, NA_W), BF16),
        scratch_shapes=[pltpu.VMEM((2, NA_W, past), BF16)],
        compiler_params=_params("parallel", "arbitrary"),
        name="na_attn",
    )(q, k, v, kt_ctx, vt_ctx, bias_rows)
    return o.reshape(batch * seq, NA_W)


def _merge_kernel(x_ref, mod_ref, of_ref, ob_ref, gr_ref, na_ref, sg_ref, ggla_ref, wg_ref, wn_ref, wo_ref,
                  gffn_ref, x1_ref, h2_ref):
    o = of_ref[...].astype(F32) + ob_ref[...].astype(F32)
    on = jnp.concatenate([_rms(o[:, h * GLA_DV:(h + 1) * GLA_DV]) for h in range(GLA_HEADS)], axis=-1)
    on = on * ggla_ref[...] * gr_ref[...].astype(F32)
    b_gla = _dot(on.astype(BF16), wg_ref[...])
    b_na = _dot(na_ref[...], wn_ref[...])
    sg = sg_ref[...].astype(F32)
    mix = _dot((sg[:, :D_MODEL] * b_gla + sg[:, D_MODEL:] * b_na).astype(BF16), wo_ref[...])
    x1 = x_ref[...] + mod_ref[2] * mix
    x1_ref[...] = x1
    h2_ref[...] = (_rms(x1) * gffn_ref[...] * (1.0 + mod_ref[4]) + mod_ref[3]).astype(BF16)


def _merge(x, mod, mod_row, of, ob, gr, na, sg, g_gla, w_gla_o, w_na_o, w_out, g_ffn):
    n = x.shape[0]
    tm = ROW_TILE
    row = lambda c: pl.BlockSpec((tm, c), lambda i: (i, 0))
    return pl.pallas_call(
        _merge_kernel,
        grid=(n // tm,),
        in_specs=[row(D_MODEL),
                  pl.BlockSpec((None, 6, 1, D_MODEL), lambda i: (mod_row(i * tm), 0, 0, 0)),
                  row(GLA_V), row(GLA_V), row(GLA_V), row(NA_W), row(2 * D_MODEL),
                  _const_spec(g_gla.shape), _const_spec(w_gla_o.shape), _const_spec(w_na_o.shape),
                  _const_spec(w_out.shape), _const_spec(g_ffn.shape)],
        out_specs=[row(D_MODEL), row(D_MODEL)],
        out_shape=[jax.ShapeDtypeStruct((n, D_MODEL), F32), jax.ShapeDtypeStruct((n, D_MODEL), BF16)],
        compiler_params=_params("parallel"),
        name="merge",
    )(x, mod, of, ob, gr, na, sg, g_gla, w_gla_o, w_na_o, w_out, g_ffn)


def _ffn_kernel(seq, h_ref, x1_ref, mod_ref, wup_ref, wc_ref, bc_ref, wdn_ref, o_ref, a_ref, g_ref):
    t = h_ref.shape[0]
    h = h_ref[...]
    pos = lax.broadcasted_iota(jnp.int32, (t, 1), 0) % seq
    first = pos == 0
    last = pos == seq - 1
    tiles = [(lo, min(lo + FF_TILE, D_FF)) for lo in range(0, D_FF, FF_TILE)]

    def up(j):
        lo, hi = tiles[j]
        a_ref[j % 2, :, :hi - lo] = _dot(h, wup_ref[:, lo:hi])
        g_ref[j % 2, :, :hi - lo] = _dot(h, wup_ref[:, D_FF + lo:D_FF + hi])

    def conv(u, lo, hi):
        prev = jnp.where(first, 0.0, pltpu.roll(u, 1, axis=0))
        nxt = jnp.where(last, 0.0, pltpu.roll(u, t - 1, axis=0))
        return wc_ref[0:1, lo:hi] * prev + wc_ref[1:2, lo:hi] * u + wc_ref[2:3, lo:hi] * nxt + bc_ref[:, lo:hi]

    up(0)
    for j, (lo, hi) in enumerate(tiles):
        if j + 1 < len(tiles):
            up(j + 1)
        a = conv(a_ref[j % 2, :, :hi - lo], lo, hi)
        g = conv(g_ref[j % 2, :, :hi - lo], D_FF + lo, D_FF + hi)
        part = _dot((a * (g * jax.nn.sigmoid(g))).astype(BF16), wdn_ref[lo:hi, :])
        if j == 0:
            o_ref[...] = part
        else:
            o_ref[...] += part
    o_ref[...] = x1_ref[...] + mod_ref[5] * o_ref[...]


def _ffn(h2, x1, mod, mod_row, w_up, w_conv, b_conv, w_down, batch, seq, seqs_per_step):
    rows = seq * seqs_per_step
    blk = lambda: pl.BlockSpec((rows, D_MODEL), lambda b: (b, 0))
    return pl.pallas_call(
        functools.partial(_ffn_kernel, seq),
        grid=(batch // seqs_per_step,),
        in_specs=[blk(), blk(),
                  pl.BlockSpec((None, 6, 1, D_MODEL), lambda b: (mod_row(b * rows), 0, 0, 0)),
                  _const_spec(w_up.shape), _const_spec(w_conv.shape), _const_spec(b_conv.shape),
                  _const_spec(w_down.shape)],
        out_specs=blk(),
        out_shape=jax.ShapeDtypeStruct((batch * seq, D_MODEL), F32),
        scratch_shapes=[pltpu.VMEM((2, rows, FF_TILE), F32), pltpu.VMEM((2, rows, FF_TILE), F32)],
        compiler_params=_params("parallel"),
        name="conv_ffn",
    )(h2, x1, mod, w_up, w_conv, b_conv, w_down)


def _layer(x, mod, mod_row, wts, ctx, gla_bt, ffn_seqs):
    batch, seq, _ = x.shape
    x2 = x.reshape(batch * seq, D_MODEL)
    q, k, v, gr, la, nq, nk, nv, sg = _inproj(x2, mod, mod_row, wts["g_mix"], wts["w_t"], wts["w_lr"],
                                              wts["b_lr"], wts["g_qn"], wts["g_kn"], wts["seg"], batch, seq,
                                              transpose_kv=ctx is None)
    s0 = None if ctx is None else (ctx[2], ctx[3])
    of, ob, sf, sb = _gla(q, k, v, la, s0, batch, seq, gla_bt)
    if ctx is None:
        o_na = _ctx_attn(nq, nk, nv, batch, seq)
    else:
        o_na = _na_attn(nq, nk, nv, ctx[0], ctx[1], ctx[4], batch, seq)
    x1, h2 = _merge(x2, mod, mod_row, of, ob, gr, o_na, sg, wts["g_gla"], wts["w_gla_o"], wts["w_na_o"],
                    wts["w_out"], wts["g_ffn"])
    y = _ffn(h2, x1, mod, mod_row, wts["w_up"], wts["w_conv"], wts["b_conv"], wts["w_down"], batch, seq,
             ffn_seqs)
    return y.reshape(batch, seq, D_MODEL), nk, nv, sf, sb


def _pack_weights(l, w_in, w_alpha_fwd, b_alpha_fwd, w_alpha_bwd, b_alpha_bwd, g_norm_mix, g_gla_norm,
                  g_q_norm, g_k_norm, w_gla_o, w_na_o, w_out, g_norm_ffn, w_up, w_conv, b_conv, w_down):
    w_t = _cast_bf16(jnp.swapaxes(w_in[l], 0, 1), CAST_ROWS)
    w_lr = jnp.zeros((LANES, 2 * GLA_QK), F32)
    w_lr = w_lr.at[:GLA_LOWRANK, :GLA_QK].set(w_alpha_fwd[l])
    w_lr = w_lr.at[GLA_LOWRANK:2 * GLA_LOWRANK, GLA_QK:].set(w_alpha_bwd[l]).astype(BF16)
    seg = jnp.asarray(np.kron(np.eye(NA_HEADS), np.ones((NA_HD, NA_HD))), BF16)
    return dict(
        g_mix=g_norm_mix[l][None], w_t=w_t, w_lr=w_lr,
        b_lr=jnp.concatenate([b_alpha_fwd[l], b_alpha_bwd[l]])[None],
        g_qn=jnp.tile(g_q_norm[l], NA_HEADS)[None], g_kn=jnp.tile(g_k_norm[l], NA_HEADS)[None], seg=seg,
        g_gla=jnp.tile(g_gla_norm[l], GLA_HEADS)[None],
        w_gla_o=w_gla_o[l].astype(BF16), w_na_o=w_na_o[l].astype(BF16), w_out=w_out[l].astype(BF16),
        g_ffn=g_norm_ffn[l][None], w_up=w_up[l].astype(BF16), w_conv=w_conv[l], b_conv=b_conv[l][None],
        w_down=w_down[l].astype(BF16))


def kernel(x_prompt, x_sample, c, cache_na_k, cache_na_v, state_gla_fwd, state_gla_bwd, c_ctx, w_ada, b_ada,
           g_norm_mix, w_in, w_alpha_fwd, b_alpha_fwd, w_alpha_bwd, b_alpha_bwd, g_gla_norm, g_q_norm, g_k_norm,
           rpb, w_gla_o, w_na_o, w_out, g_norm_ffn, w_up, w_conv, b_conv, w_down):
    depth = w_in.shape[0]
    batch, seq, _ = x_prompt.shape
    dec_batch, dec_seq, _ = x_sample.shape
    past = cache_na_k.shape[2]
    cond = jnp.concatenate([c_ctx[None], c], axis=0)
    xp, xs = x_prompt, x_sample
    new_k, new_v, new_sf, new_sb = [], [], [], []
    for l in range(depth):
        wts = _pack_weights(l, w_in, w_alpha_fwd, b_alpha_fwd, w_alpha_bwd, b_alpha_bwd, g_norm_mix, g_gla_norm,
                            g_q_norm, g_k_norm, w_gla_o, w_na_o, w_out, g_norm_ffn, w_up, w_conv, b_conv, w_down)
        mod = _ada(cond, w_ada[l], b_ada[l][None]).reshape(1 + dec_batch, 6, 1, D_MODEL)
        tbl = _bias_rows(rpb[l])
        xp, kc, vc, sf, sb = _layer(xp, mod, lambda tok: 0, wts, None, gla_bt=4, ffn_seqs=2)
        to_cache = lambda a: a.reshape(batch, NA_HEADS, NA_HD, seq).transpose(0, 3, 1, 2)
        from_cache = lambda a: a.transpose(0, 2, 3, 1).reshape(dec_batch, NA_W, past)
        new_k.append(to_cache(kc))
        new_v.append(to_cache(vc))
        new_sf.append(sf)
        new_sb.append(sb)
        ctx = (from_cache(cache_na_k[:, l]), from_cache(cache_na_v[:, l]),
               state_gla_fwd[:, l], state_gla_bwd[:, l], tbl)
        xs, _, _, _, _ = _layer(xs, mod, lambda tok: 1 + tok // dec_seq, wts, ctx, gla_bt=dec_batch,
                               ffn_seqs=1)
    return (xp, xs, jnp.stack(new_k, axis=1), jnp.stack(new_v, axis=1),
            jnp.stack(new_sf, axis=1), jnp.stack(new_sb, axis=1))
```

```python
import functools

import numpy as np
import jax
import jax.numpy as jnp
from jax import lax
from jax.experimental import pallas as pl
from jax.experimental.pallas import tpu as pltpu

F32 = jnp.float32
BF16 = jnp.bfloat16

D_MODEL = 1024
GRID_W = 64
GLA_HEADS, GLA_DK, GLA_DV = 4, 64, 128
GLA_QK = GLA_HEADS * GLA_DK
GLA_V = GLA_HEADS * GLA_DV
GLA_LOWRANK = 16
GLA_GATE_NORM = 16.0
GLA_CHUNK = 64
NA_HEADS, NA_HD = 8, 64
NA_W = NA_HEADS * NA_HD
NA_WIN_ROWS, NA_WIN_COLS = 8, 16
D_FF = 2816
EPS = 1e-6
IN_SIZES = (GLA_QK, GLA_QK, GLA_V, GLA_V, GLA_LOWRANK, GLA_LOWRANK, NA_W, NA_W, NA_W, D_MODEL, D_MODEL)

LANES = 128
ROW_TILE = 256
FF_TILE = 512
MASK_VALUE = -1e30
VMEM_LIMIT = 56 * 1024 * 1024

NT_DIMS = (((1,), (1,)), ((), ()))
TN_DIMS = (((0,), (0,)), ((), ()))


def _dot(a, b):
    return jnp.dot(a, b, preferred_element_type=F32)


def _dot_nt(a, b):
    return lax.dot_general(a, b, NT_DIMS, preferred_element_type=F32)


def _dot_tn(a, b):
    return lax.dot_general(a, b, TN_DIMS, preferred_element_type=F32)


def _split3(a):
    t1 = a.astype(BF16)
    e1 = a - t1.astype(F32)
    t2 = e1.astype(BF16)
    t3 = (e1 - t2.astype(F32)).astype(BF16)
    return t1, t2, t3


def _params(*sem):
    return pltpu.CompilerParams(dimension_semantics=sem, vmem_limit_bytes=VMEM_LIMIT)


def _const_spec(shape):
    zeros = (0,) * len(shape)
    return pl.BlockSpec(shape, lambda *_: zeros, pipeline_mode=pl.Buffered(1))


def _ada_kernel(c_ref, w_ref, b_ref, o_ref):
    c = c_ref[...]
    s = c * jax.nn.sigmoid(c)
    o_ref[...] = _dot(s.astype(BF16), w_ref[...].astype(BF16)) + b_ref[...]


def _ada(cond, w_ada, b_ada):
    n = cond.shape[0]
    return pl.pallas_call(
        _ada_kernel,
        grid=(6,),
        in_specs=[pl.BlockSpec((n, D_MODEL), lambda j: (0, 0)),
                  pl.BlockSpec((D_MODEL, D_MODEL), lambda j: (0, j)),
                  pl.BlockSpec((1, D_MODEL), lambda j: (0, j))],
        out_specs=pl.BlockSpec((n, D_MODEL), lambda j: (0, j)),
        out_shape=jax.ShapeDtypeStruct((n, 6 * D_MODEL), F32),
        compiler_params=_params("arbitrary"),
        name="ada",
    )(cond, w_ada, b_ada)


def _rms(x):
    return x * lax.rsqrt(jnp.mean(x * x, axis=-1, keepdims=True) + EPS)


_W_OFF = [int(o) for o in np.cumsum((0,) + IN_SIZES)]
W_GLA, W_LR, W_NA, W_MERGE, W_END = _W_OFF[0], _W_OFF[4], _W_OFF[6], _W_OFF[9], _W_OFF[11]
CAST_ROWS = 736


def _cast_kernel(w_ref, o_ref):
    o_ref[...] = w_ref[...].astype(BF16)


def _cast_bf16(w, rows):
    n, c = w.shape
    assert n % rows == 0
    return pl.pallas_call(
        _cast_kernel,
        grid=(n // rows,),
        in_specs=[pl.BlockSpec((rows, c), lambda i: (i, 0))],
        out_specs=pl.BlockSpec((rows, c), lambda i: (i, 0)),
        out_shape=jax.ShapeDtypeStruct((n, c), BF16),
        compiler_params=_params("parallel"),
        name="cast_bf16",
    )(w)


def _inproj_kernel(transpose_kv, x_ref, mod_ref, g_ref, w_ref, wlr_ref, blr_ref, gqn_ref, gkn_ref, seg_ref,
                   q_ref, k_ref, v_ref, gr_ref, la_ref, nq_ref, nk_ref, nv_ref, sg_ref):
    h = _rms(x_ref[...]) * g_ref[...]
    h = (h * (1.0 + mod_ref[1]) + mod_ref[0]).astype(BF16)

    u = _dot_nt(h, w_ref[W_GLA:W_LR, :])
    q_ref[...] = (u[:, :GLA_QK] * (GLA_DK ** -0.5)).astype(BF16)
    k_ref[...] = u[:, GLA_QK:2 * GLA_QK].astype(BF16)
    v_ref[...] = u[:, 2 * GLA_QK:2 * GLA_QK + GLA_V].astype(BF16)
    gr = u[:, 2 * GLA_QK + GLA_V:]
    gr_ref[...] = (gr * jax.nn.sigmoid(gr)).astype(BF16)

    z = _dot(_dot_nt(h, w_ref[W_LR:W_LR + LANES, :]).astype(BF16), wlr_ref[...]) + blr_ref[...]
    log_sig = jnp.minimum(z, 0.0) - jnp.log1p(jnp.exp(-jnp.abs(z)))
    la_ref[...] = log_sig * (1.0 / GLA_GATE_NORM)

    u = _dot_nt(h, w_ref[W_NA:W_MERGE, :])

    def head_norm(a, gain):
        ssq = _dot((a * a).astype(BF16), seg_ref[...])
        return a * lax.rsqrt(ssq * (1.0 / NA_HD) + EPS) * gain

    nq_ref[...] = (head_norm(u[:, :NA_W], gqn_ref[...]) * (NA_HD ** -0.5)).astype(BF16)
    nk = head_norm(u[:, NA_W:2 * NA_W], gkn_ref[...])
    nv = u[:, 2 * NA_W:]
    if transpose_kv:
        nk_ref[...] = nk.T
        nv_ref[...] = nv.T
    else:
        nk_ref[...] = nk.astype(nk_ref.dtype)
        nv_ref[...] = nv.astype(nv_ref.dtype)

    sg_ref[...] = jax.nn.sigmoid(_dot_nt(h, w_ref[W_MERGE:W_END, :])).astype(BF16)


def _inproj(x, mod, mod_row, g_mix, w_t, w_lr, b_lr, g_qn, g_kn, seg, batch, seq, transpose_kv):
    n = x.shape[0]
    tm = ROW_TILE
    assert seq % tm == 0 and W_LR % 16 == 0 and W_NA % 16 == 0 and W_MERGE % 16 == 0
    row = lambda c: pl.BlockSpec((tm, c), lambda i: (i, 0))
    outs = [(GLA_QK, BF16), (GLA_QK, BF16), (GLA_V, BF16), (GLA_V, BF16), (2 * GLA_QK, F32), (NA_W, BF16)]
    out_specs = [row(c) for c, _ in outs]
    out_shape = [jax.ShapeDtypeStruct((n, c), dt) for c, dt in outs]
    if transpose_kv:
        tps = seq // tm
        kv_spec = pl.BlockSpec((None, NA_W, tm), lambda i: (i // tps, 0, i % tps))
        kv_shape = jax.ShapeDtypeStruct((batch, NA_W, seq), F32)
    else:
        kv_spec = row(NA_W)
        kv_shape = jax.ShapeDtypeStruct((n, NA_W), BF16)
    out_specs += [kv_spec, kv_spec, row(2 * D_MODEL)]
    out_shape += [kv_shape, kv_shape, jax.ShapeDtypeStruct((n, 2 * D_MODEL), BF16)]
    return pl.pallas_call(
        functools.partial(_inproj_kernel, transpose_kv),
        grid=(n // tm,),
        in_specs=[row(D_MODEL),
                  pl.BlockSpec((None, 6, 1, D_MODEL), lambda i: (mod_row(i * tm), 0, 0, 0)),
                  _const_spec(g_mix.shape), _const_spec(w_t.shape), _const_spec(w_lr.shape),
                  _const_spec(b_lr.shape), _const_spec(g_qn.shape), _const_spec(g_kn.shape),
                  _const_spec(seg.shape)],
        out_specs=out_specs,
        out_shape=out_shape,
        compiler_params=_params("parallel"),
        name="inproj",
    )(x, mod, g_mix, w_t, w_lr, b_lr, g_qn, g_kn, seg)


GLA_SAFE_DECAY = 80.0


def _chunk_cumsum(x, reverse):
    c = x.shape[0]
    rows = lax.broadcasted_iota(jnp.int32, (c, 1), 0)
    s = 1
    while s < c:
        if reverse:
            x = x + jnp.where(rows < c - s, pltpu.roll(x, c - s, axis=0), 0.0)
        else:
            x = x + jnp.where(rows >= s, pltpu.roll(x, s, axis=0), 0.0)
        s *= 2
    return x


def _gla_kernel(has_s0, nchunks, *refs):
    qf, kf, vf, laf, qb, kb, vb, lab = refs[:8]
    if has_s0:
        s0f, s0b, of, ob, sf, sb, st, oi, q32, bcs = refs[8:]
    else:
        of, ob, sf, sb, st, oi, q32, bcs = refs[8:]
        s0f = s0b = None
    n = pl.program_id(1)
    bt = qf.shape[0]
    C, H, DK, DV = GLA_CHUNK, GLA_HEADS, GLA_DK, GLA_DV
    assert C == DK
    HC = H * C
    row = lax.broadcasted_iota(jnp.int32, (HC, HC), 0)
    col = lax.broadcasted_iota(jnp.int32, (HC, HC), 1)
    same_head = (row // C) == (col // C)
    t_in, s_in = row % C, col % C
    vis = (same_head & (t_in >= s_in), same_head & (t_in <= s_in))
    ins = ((qf, kf, vf, laf, of), (qb, kb, vb, lab, ob))
    units = [(d, b) for d in range(2) for b in range(bt)]
    heads = [(slice(h * DK, (h + 1) * DK), slice(h * DV, (h + 1) * DV)) for h in range(H)]

    def rows4(a):
        return jnp.concatenate([a] * H, axis=0)

    def block_diag(a):
        return jnp.where(same_head, rows4(a), jnp.zeros((), a.dtype))

    def v_stack(v):
        return jnp.concatenate([v[:, vs] for _, vs in heads], axis=0)

    @pl.when(n == 0)
    def _init():
        for d, s0 in enumerate((s0f, s0b)):
            for b in range(bt):
                for h in range(H):
                    st[d, b, h * DK:(h + 1) * DK, :] = s0[b, h] if has_s0 else jnp.zeros((DK, DV), F32)

    bc, tot, q_bd, kh_bd, et = {}, {}, {}, {}, {}
    for u in units:
        d, b = u
        bc[u] = _chunk_cumsum(ins[d][3][b], reverse=d == 1)
        tot[u] = bc[u][C - 1:C, :] if d == 0 else bc[u][0:1, :]
    worst = functools.reduce(jnp.maximum, [jnp.max(-tot[u]) for u in units])
    safe = worst <= GLA_SAFE_DECAY

    for u in units:
        d, b = u
        q = ins[d][0][b].astype(F32)
        k = ins[d][1][b].astype(F32)
        q_bd[u] = block_diag((q * jnp.exp(bc[u])).astype(BF16))
        kh_bd[u] = block_diag((k * jnp.exp(tot[u] - bc[u])).astype(BF16))
        et[u] = jnp.exp(tot[u])

    @pl.when(safe)
    def _intra_factorised():
        for u in units:
            d, b = u
            k = ins[d][1][b].astype(F32)
            kt = rows4((k * jnp.exp(-bc[u])).astype(BF16))
            s = jnp.where(vis[d], _dot_nt(q_bd[u], kt), 0.0).astype(BF16)
            oi[d, b] = _dot(s, v_stack(ins[d][2][b]))

    @pl.when(jnp.logical_not(safe))
    def _intra_pairwise():
        for u in units:
            d, b = u
            q32[d, b] = ins[d][0][b].astype(F32)
            bcs[d, b] = bc[u]
        svis = lax.broadcasted_iota(jnp.int32, (C, 1), 0)
        for u in units:
            d, b = u
            k = ins[d][1][b].astype(F32)
            v = ins[d][2][b].astype(F32)
            for h, (ks, vs) in enumerate(heads):
                kh_, bh, vh = k[:, ks], bc[u][:, ks], v[:, vs]

                def one_row(t, carry, d=d, b=b, h=h, ks=ks, kh_=kh_, bh=bh, vh=vh):
                    bt_row = bcs[d, b, pl.ds(t, 1), :][:, ks]
                    q_row = q32[d, b, pl.ds(t, 1), :][:, ks]
                    seen = (svis <= t) if d == 0 else (svis >= t)
                    w = jnp.exp(jnp.where(seen, bt_row - bh, -jnp.inf)) * kh_ * q_row
                    s_col = jnp.sum(w, axis=1, keepdims=True)
                    oi[d, b, pl.ds(h * C + t, 1), :] = jnp.sum(s_col * vh, axis=0, keepdims=True)
                    return carry

                lax.fori_loop(0, C, one_row, 0)

    for u in units:
        d, b = u
        o = oi[d, b] + _dot(q_bd[u], st[d, b].astype(BF16))
        ins[d][4][b] = jnp.concatenate([o[h * C:(h + 1) * C] for h in range(H)], axis=1).astype(ins[d][4].dtype)
    ones = jnp.ones((16, DV), BF16)
    for u in units:
        d, b = u
        e_rows = jnp.concatenate(list(_split3(et[u])) + [jnp.zeros((13, H * DK), BF16)], axis=0)
        decay = _dot_tn(e_rows, ones)
        st[d, b] = decay * st[d, b] + _dot_tn(kh_bd[u], v_stack(ins[d][2][b]))

    @pl.when(n == nchunks - 1)
    def _fin():
        for b in range(bt):
            for h in range(H):
                sf[b, h] = st[0, b, h * DK:(h + 1) * DK, :]
                sb[b, h] = st[1, b, h * DK:(h + 1) * DK, :]


def _gla(q, k, v, la, s0, batch, seq, bt):
    C = GLA_CHUNK
    nc = seq // C
    q, k, v, la = (a.reshape(batch, seq, a.shape[-1]) for a in (q, k, v, la))
    fwd = lambda c, j=0: pl.BlockSpec((bt, C, c), lambda i, n: (i, n, j))
    bwd = lambda c, j=0: pl.BlockSpec((bt, C, c), lambda i, n: (i, nc - 1 - n, j))
    st_spec = pl.BlockSpec((bt, GLA_HEADS, GLA_DK, GLA_DV), lambda i, n: (i, 0, 0, 0))
    in_specs = [fwd(GLA_QK), fwd(GLA_QK), fwd(GLA_V), fwd(GLA_QK, 0),
                bwd(GLA_QK), bwd(GLA_QK), bwd(GLA_V), bwd(GLA_QK, 1)]
    args = [q, k, v, la, q, k, v, la]
    if s0 is not None:
        in_specs += [st_spec, st_spec]
        args += list(s0)
    st_shape = jax.ShapeDtypeStruct((batch, GLA_HEADS, GLA_DK, GLA_DV), F32)
    o_shape = jax.ShapeDtypeStruct((batch, seq, GLA_V), BF16)
    of, ob, sf, sb = pl.pallas_call(
        functools.partial(_gla_kernel, s0 is not None, nc),
        grid=(batch // bt, nc),
        in_specs=in_specs,
        out_specs=[fwd(GLA_V), bwd(GLA_V), st_spec, st_spec],
        out_shape=[o_shape, o_shape, st_shape, st_shape],
        scratch_shapes=[pltpu.VMEM((2, bt, GLA_QK, GLA_DV), F32),
                        pltpu.VMEM((2, bt, GLA_HEADS * C, GLA_DV), F32),
                        pltpu.VMEM((2, bt, C, GLA_QK), F32),
                        pltpu.VMEM((2, bt, C, GLA_QK), F32)],
        compiler_params=_params("parallel", "arbitrary"),
        name="gla",
    )(*args)
    return of.reshape(batch * seq, GLA_V), ob.reshape(batch * seq, GLA_V), sf, sb


def _softmax_pv(scores, values):
    m = functools.reduce(jnp.maximum, [jnp.max(s, axis=-1, keepdims=True) for s in scores])
    ps = [jnp.exp(s - m) for s in scores]
    l = functools.reduce(jnp.add, [jnp.sum(p, axis=-1, keepdims=True) for p in ps])
    o = functools.reduce(jnp.add, [(_dot_nt if transposed else _dot)(p.astype(BF16), v)
                                   for p, (v, transposed) in zip(ps, values)])
    return o / l


def _ctx_attn_kernel(q_ref, kt_ref, vt_ref, o_ref):
    q = q_ref[...]
    kt = kt_ref[...].astype(BF16)
    vt = vt_ref[...].astype(BF16)
    outs = []
    for h in range(NA_HEADS):
        hs = slice(h * NA_HD, (h + 1) * NA_HD)
        outs.append(_softmax_pv([_dot(q[:, hs], kt[hs, :])], [(vt[hs, :], True)]))
    o_ref[...] = jnp.concatenate(outs, axis=-1).astype(o_ref.dtype)


def _ctx_attn(q, kt, vt, batch, seq):
    spec = pl.BlockSpec((seq, NA_W), lambda b: (b, 0))
    spec_t = pl.BlockSpec((None, NA_W, seq), lambda b: (b, 0, 0))
    return pl.pallas_call(
        _ctx_attn_kernel,
        grid=(batch,),
        in_specs=[spec, spec_t, spec_t],
        out_specs=spec,
        out_shape=jax.ShapeDtypeStruct((batch * seq, NA_W), BF16),
        compiler_params=_params("parallel"),
        name="ctx_attn",
    )(q, kt, vt)


N_DR = 2 * NA_WIN_ROWS - 1
N_DC = 2 * NA_WIN_COLS - 1
NA_QROWS = 4
NA_KROWS = NA_WIN_ROWS + NA_QROWS
BIAS_BOTH = 0
BIAS_RIGHT = N_DR - 1
BIAS_LEFT = BIAS_RIGHT + NA_WIN_ROWS
BIAS_NONE = BIAS_LEFT + NA_WIN_ROWS
BIAS_ROWS = 32


def _bias_rows(rpb):
    assert 2 * GRID_W == LANES and BIAS_NONE < BIAS_ROWS
    b = jnp.pad(rpb, ((0, 0), (0, 0), (0, GRID_W - N_DC)))
    m = jnp.full((NA_HEADS, NA_WIN_ROWS, GRID_W), MASK_VALUE, F32)
    both = jnp.concatenate([b[:, :-1], b[:, 1:]], axis=-1)
    right = jnp.concatenate([m, b[:, :NA_WIN_ROWS]], axis=-1)
    left = jnp.concatenate([b[:, NA_WIN_ROWS - 1:], m], axis=-1)
    none = jnp.concatenate([m[:, :1], m[:, :1]], axis=-1)
    pad = jnp.zeros((NA_HEADS, BIAS_ROWS - BIAS_NONE - 1, LANES), F32)
    return jnp.concatenate([both, right, left, none, pad], axis=1).reshape(NA_HEADS * BIAS_ROWS, LANES)


def _na_kernel(rows, q_ref, k_ref, v_ref, kct_ref, vct_ref, bias_ref, o_ref, ctx_ref):
    g = pl.program_id(1)

    @pl.when(g == 0)
    def _cast_ctx():
        ctx_ref[0] = kct_ref[...].astype(BF16)
        ctx_ref[1] = vct_ref[...].astype(BF16)

    wr = NA_WIN_ROWS
    ws = jnp.clip(g * NA_QROWS - wr // 2, 0, rows - NA_KROWS)
    start = pl.multiple_of(ws * GRID_W, GRID_W)
    q = q_ref[...]
    kl = k_ref[pl.ds(start, NA_KROWS * GRID_W), :]
    vl = v_ref[pl.ds(start, NA_KROWS * GRID_W), :]

    tile_row = []
    for i in range(NA_QROWS):
        r = g * NA_QROWS + i
        r0 = jnp.clip(r - wr // 2, 0, rows - wr)
        for j in range(NA_KROWS // 2):
            kr = ws + 2 * j
            in0 = (kr >= r0) & (kr < r0 + wr)
            in1 = (kr + 1 >= r0) & (kr + 1 < r0 + wr)
            dr = kr - r + wr - 1
            tile_row.append(jnp.where(in0 & in1, BIAS_BOTH + dr,
                                      jnp.where(in1, BIAS_RIGHT + dr + 1,
                                                jnp.where(in0, BIAS_LEFT + dr - (wr - 1), BIAS_NONE))))

    qc = lax.broadcasted_iota(jnp.int32, (GRID_W, LANES), 0)
    kc = lax.broadcasted_iota(jnp.int32, (GRID_W, LANES), 1) % GRID_W
    c0 = jnp.clip(qc - NA_WIN_COLS // 2, 0, GRID_W - NA_WIN_COLS)
    in_win = (kc >= c0) & (kc < c0 + NA_WIN_COLS)
    shift = LANES - (NA_WIN_COLS - 1)

    def bias_tile(h, t):
        row = jnp.broadcast_to(bias_ref[pl.ds(h * BIAS_ROWS + t, 1), :], (GRID_W, LANES))
        return jnp.where(in_win, pltpu.roll(row, shift, axis=1, stride=1, stride_axis=0), MASK_VALUE)

    outs = []
    for h in range(NA_HEADS):
        hs = slice(h * NA_HD, (h + 1) * NA_HD)
        bias = jnp.concatenate(
            [jnp.concatenate([bias_tile(h, tile_row[i * (NA_KROWS // 2) + j]) for j in range(NA_KROWS // 2)], axis=1)
             for i in range(NA_QROWS)], axis=0)
        s_loc = _dot_nt(q[:, hs], kl[:, hs]) + bias
        s_ctx = _dot(q[:, hs], ctx_ref[0, hs, :])
        outs.append(_softmax_pv([s_loc, s_ctx], [(vl[:, hs], False), (ctx_ref[1, hs, :], True)]))
    o_ref[...] = jnp.concatenate(outs, axis=-1).astype(o_ref.dtype)


def _na_attn(q, k, v, kt_ctx, vt_ctx, bias_rows, batch, seq):
    rows = seq // GRID_W
    wr = NA_WIN_ROWS
    assert rows >= NA_KROWS and rows % NA_QROWS == 0 and wr % 2 == 0 and NA_KROWS % 2 == 0
    for g in range(rows // NA_QROWS):
        ws = min(max(g * NA_QROWS - wr // 2, 0), rows - NA_KROWS)
        for r in range(g * NA_QROWS, (g + 1) * NA_QROWS):
            r0 = min(max(r - wr // 2, 0), rows - wr)
            assert ws <= r0 and r0 + wr <= ws + NA_KROWS
    past = kt_ctx.shape[2]
    q, k, v = (a.reshape(batch, seq, NA_W) for a in (q, k, v))
    whole = pl.BlockSpec((None, seq, NA_W), lambda b, g: (b, 0, 0))
    whole_t = pl.BlockSpec((None, NA_W, past), lambda b, g: (b, 0, 0))
    tile = pl.BlockSpec((None, NA_QROWS * GRID_W, NA_W), lambda b, g: (b, g, 0))
    o = pl.pallas_call(
        functools.partial(_na_kernel, rows),
        grid=(batch, rows // NA_QROWS),
        in_specs=[tile, whole, whole, whole_t, whole_t, _const_spec(bias_rows.shape)],
        out_specs=tile,
        out_shape=jax.ShapeDtypeStruct((batch, seq, NA_W), BF16),
        scratch_shapes=[pltpu.VMEM((2, NA_W, past), BF16)],
        compiler_params=_params("parallel", "arbitrary"),
        name="na_attn",
    )(q, k, v, kt_ctx, vt_ctx, bias_rows)
    return o.reshape(batch * seq, NA_W)


def _merge_body(x, mod_ref, of_ref, ob_ref, gr_ref, na_ref, sg_ref, ggla_ref, wg_ref, wn_ref, wo_ref, gffn_ref):
    o = of_ref[...].astype(F32) + ob_ref[...].astype(F32)
    on = jnp.concatenate([_rms(o[:, h * GLA_DV:(h + 1) * GLA_DV]) for h in range(GLA_HEADS)], axis=-1)
    on = on * ggla_ref[...] * gr_ref[...].astype(F32)
    b_gla = _dot(on.astype(BF16), wg_ref[...])
    b_na = _dot(na_ref[...], wn_ref[...])
    sg = sg_ref[...].astype(F32)
    mix = _dot((sg[:, :D_MODEL] * b_gla + sg[:, D_MODEL:] * b_na).astype(BF16), wo_ref[...])
    x1 = x + mod_ref[2] * mix
    h2 = (_rms(x1) * gffn_ref[...] * (1.0 + mod_ref[4]) + mod_ref[3]).astype(BF16)
    return x1, h2


def _merge_kernel(x_ref, mod_ref, of_ref, ob_ref, gr_ref, na_ref, sg_ref, ggla_ref, wg_ref, wn_ref, wo_ref,
                  gffn_ref, x1_ref, h2_ref):
    x1_ref[...], h2_ref[...] = _merge_body(x_ref[...], mod_ref, of_ref, ob_ref, gr_ref, na_ref, sg_ref, ggla_ref,
                                           wg_ref, wn_ref, wo_ref, gffn_ref)


def _merge(x, mod, mod_row, of, ob, gr, na, sg, g_gla, w_gla_o, w_na_o, w_out, g_ffn):
    n = x.shape[0]
    tm = ROW_TILE
    row = lambda c: pl.BlockSpec((tm, c), lambda i: (i, 0))
    return pl.pallas_call(
        _merge_kernel,
        grid=(n // tm,),
        in_specs=[row(D_MODEL),
                  pl.BlockSpec((None, 6, 1, D_MODEL), lambda i: (mod_row(i * tm), 0, 0, 0)),
                  row(GLA_V), row(GLA_V), row(GLA_V), row(NA_W), row(2 * D_MODEL),
                  _const_spec(g_gla.shape), _const_spec(w_gla_o.shape), _const_spec(w_na_o.shape),
                  _const_spec(w_out.shape), _const_spec(g_ffn.shape)],
        out_specs=[row(D_MODEL), row(D_MODEL)],
        out_shape=[jax.ShapeDtypeStruct((n, D_MODEL), F32), jax.ShapeDtypeStruct((n, D_MODEL), BF16)],
        compiler_params=_params("parallel"),
        name="merge",
    )(x, mod, of, ob, gr, na, sg, g_gla, w_gla_o, w_na_o, w_out, g_ffn)


def _ffn_body(seq, h, wup_ref, wc_ref, bc_ref, wdn_ref, acc_ref, a_ref, g_ref):
    t = h.shape[0]
    pos = lax.broadcasted_iota(jnp.int32, (t, 1), 0) % seq
    first = pos == 0
    last = pos == seq - 1
    tiles = [(lo, min(lo + FF_TILE, D_FF)) for lo in range(0, D_FF, FF_TILE)]

    def up(j):
        lo, hi = tiles[j]
        a_ref[j % 2, :, :hi - lo] = _dot(h, wup_ref[:, lo:hi])
        g_ref[j % 2, :, :hi - lo] = _dot(h, wup_ref[:, D_FF + lo:D_FF + hi])

    def conv(u, lo, hi):
        prev = jnp.where(first, 0.0, pltpu.roll(u, 1, axis=0))
        nxt = jnp.where(last, 0.0, pltpu.roll(u, t - 1, axis=0))
        return wc_ref[0:1, lo:hi] * prev + wc_ref[1:2, lo:hi] * u + wc_ref[2:3, lo:hi] * nxt + bc_ref[:, lo:hi]

    up(0)
    for j, (lo, hi) in enumerate(tiles):
        if j + 1 < len(tiles):
            up(j + 1)
        a = conv(a_ref[j % 2, :, :hi - lo], lo, hi)
        g = conv(g_ref[j % 2, :, :hi - lo], D_FF + lo, D_FF + hi)
        part = _dot((a * (g * jax.nn.sigmoid(g))).astype(BF16), wdn_ref[lo:hi, :])
        if j == 0:
            acc_ref[...] = part
        else:
            acc_ref[...] += part


def _ffn_kernel(seq, h_ref, x1_ref, mod_ref, wup_ref, wc_ref, bc_ref, wdn_ref, o_ref, a_ref, g_ref):
    _ffn_body(seq, h_ref[...], wup_ref, wc_ref, bc_ref, wdn_ref, o_ref, a_ref, g_ref)
    o_ref[...] = x1_ref[...] + mod_ref[5] * o_ref[...]


def _merge_ffn_kernel(seq, x_ref, mod_ref, of_ref, ob_ref, gr_ref, na_ref, sg_ref, ggla_ref, wg_ref, wn_ref, wo_ref,
                      gffn_ref, wup_ref, wc_ref, bc_ref, wdn_ref, o_ref, x1_ref, a_ref, g_ref):
    x1_ref[...], h2 = _merge_body(x_ref[...], mod_ref, of_ref, ob_ref, gr_ref, na_ref, sg_ref, ggla_ref,
                                  wg_ref, wn_ref, wo_ref, gffn_ref)
    _ffn_body(seq, h2, wup_ref, wc_ref, bc_ref, wdn_ref, o_ref, a_ref, g_ref)
    o_ref[...] = x1_ref[...] + mod_ref[5] * o_ref[...]


def _merge_ffn(x, mod, mod_row, of, ob, gr, na, sg, g_gla, w_gla_o, w_na_o, w_out, g_ffn, w_up, w_conv, b_conv,
               w_down, batch, seq, seqs_per_step):
    rows = seq * seqs_per_step
    row = lambda c: pl.BlockSpec((rows, c), lambda i: (i, 0))
    consts = (g_gla, w_gla_o, w_na_o, w_out, g_ffn, w_up, w_conv, b_conv, w_down)
    return pl.pallas_call(
        functools.partial(_merge_ffn_kernel, seq),
        grid=(batch // seqs_per_step,),
        in_specs=[row(D_MODEL),
                  pl.BlockSpec((None, 6, 1, D_MODEL), lambda i: (mod_row(i * rows), 0, 0, 0)),
                  row(GLA_V), row(GLA_V), row(GLA_V), row(NA_W), row(2 * D_MODEL)]
                 + [_const_spec(c.shape) for c in consts],
        out_specs=row(D_MODEL),
        out_shape=jax.ShapeDtypeStruct((batch * seq, D_MODEL), F32),
        scratch_shapes=[pltpu.VMEM((rows, D_MODEL), F32),
                        pltpu.VMEM((2, rows, FF_TILE), F32), pltpu.VMEM((2, rows, FF_TILE), F32)],
        compiler_params=_params("parallel"),
        name="merge_ffn",
    )(x, mod, of, ob, gr, na, sg, *consts)


def _ffn(h2, x1, mod, mod_row, w_up, w_conv, b_conv, w_down, batch, seq, seqs_per_step):
    rows = seq * seqs_per_step
    blk = lambda: pl.BlockSpec((rows, D_MODEL), lambda b: (b, 0))
    return pl.pallas_call(
        functools.partial(_ffn_kernel, seq),
        grid=(batch // seqs_per_step,),
        in_specs=[blk(), blk(),
                  pl.BlockSpec((None, 6, 1, D_MODEL), lambda b: (mod_row(b * rows), 0, 0, 0)),
                  _const_spec(w_up.shape), _const_spec(w_conv.shape), _const_spec(b_conv.shape),
                  _const_spec(w_down.shape)],
        out_specs=blk(),
        out_shape=jax.ShapeDtypeStruct((batch * seq, D_MODEL), F32),
        scratch_shapes=[pltpu.VMEM((2, rows, FF_TILE), F32), pltpu.VMEM((2, rows, FF_TILE), F32)],
        compiler_params=_params("parallel"),
        name="conv_ffn",
    )(h2, x1, mod, w_up, w_conv, b_conv, w_down)


def _layer(x, mod, mod_row, wts, ctx, gla_bt, ffn_seqs, fuse_merge_ffn):
    batch, seq, _ = x.shape
    x2 = x.reshape(batch * seq, D_MODEL)
    q, k, v, gr, la, nq, nk, nv, sg = _inproj(x2, mod, mod_row, wts["g_mix"], wts["w_t"], wts["w_lr"],
                                              wts["b_lr"], wts["g_qn"], wts["g_kn"], wts["seg"], batch, seq,
                                              transpose_kv=ctx is None)
    s0 = None if ctx is None else (ctx[2], ctx[3])
    of, ob, sf, sb = _gla(q, k, v, la, s0, batch, seq, gla_bt)
    if ctx is None:
        o_na = _ctx_attn(nq, nk, nv, batch, seq)
    else:
        o_na = _na_attn(nq, nk, nv, ctx[0], ctx[1], ctx[4], batch, seq)
    merge_w = (wts["g_gla"], wts["w_gla_o"], wts["w_na_o"], wts["w_out"], wts["g_ffn"])
    ffn_w = (wts["w_up"], wts["w_conv"], wts["b_conv"], wts["w_down"])
    if fuse_merge_ffn:
        y = _merge_ffn(x2, mod, mod_row, of, ob, gr, o_na, sg, *merge_w, *ffn_w, batch, seq, ffn_seqs)
    else:
        x1, h2 = _merge(x2, mod, mod_row, of, ob, gr, o_na, sg, *merge_w)
        y = _ffn(h2, x1, mod, mod_row, *ffn_w, batch, seq, ffn_seqs)
    return y.reshape(batch, seq, D_MODEL), nk, nv, sf, sb


def _pack_weights(l, w_in, w_alpha_fwd, b_alpha_fwd, w_alpha_bwd, b_alpha_bwd, g_norm_mix, g_gla_norm,
                  g_q_norm, g_k_norm, w_gla_o, w_na_o, w_out, g_norm_ffn, w_up, w_conv, b_conv, w_down):
    w_t = _cast_bf16(jnp.swapaxes(w_in[l], 0, 1), CAST_ROWS)
    w_lr = jnp.zeros((LANES, 2 * GLA_QK), F32)
    w_lr = w_lr.at[:GLA_LOWRANK, :GLA_QK].set(w_alpha_fwd[l])
    w_lr = w_lr.at[GLA_LOWRANK:2 * GLA_LOWRANK, GLA_QK:].set(w_alpha_bwd[l]).astype(BF16)
    seg = jnp.asarray(np.kron(np.eye(NA_HEADS), np.ones((NA_HD, NA_HD))), BF16)
    return dict(
        g_mix=g_norm_mix[l][None], w_t=w_t, w_lr=w_lr,
        b_lr=jnp.concatenate([b_alpha_fwd[l], b_alpha_bwd[l]])[None],
        g_qn=jnp.tile(g_q_norm[l], NA_HEADS)[None], g_kn=jnp.tile(g_k_norm[l], NA_HEADS)[None], seg=seg,
        g_gla=jnp.tile(g_gla_norm[l], GLA_HEADS)[None],
        w_gla_o=w_gla_o[l].astype(BF16), w_na_o=w_na_o[l].astype(BF16), w_out=w_out[l].astype(BF16),
        g_ffn=g_norm_ffn[l][None], w_up=w_up[l].astype(BF16), w_conv=w_conv[l], b_conv=b_conv[l][None],
        w_down=w_down[l].astype(BF16))


def kernel(x_prompt, x_sample, c, cache_na_k, cache_na_v, state_gla_fwd, state_gla_bwd, c_ctx, w_ada, b_ada,
           g_norm_mix, w_in, w_alpha_fwd, b_alpha_fwd, w_alpha_bwd, b_alpha_bwd, g_gla_norm, g_q_norm, g_k_norm,
           rpb, w_gla_o, w_na_o, w_out, g_norm_ffn, w_up, w_conv, b_conv, w_down):
    depth = w_in.shape[0]
    batch, seq, _ = x_prompt.shape
    dec_batch, dec_seq, _ = x_sample.shape
    past = cache_na_k.shape[2]
    cond = jnp.concatenate([c_ctx[None], c], axis=0)
    xp, xs = x_prompt, x_sample
    new_k, new_v, new_sf, new_sb = [], [], [], []
    for l in range(depth):
        wts = _pack_weights(l, w_in, w_alpha_fwd, b_alpha_fwd, w_alpha_bwd, b_alpha_bwd, g_norm_mix, g_gla_norm,
                            g_q_norm, g_k_norm, w_gla_o, w_na_o, w_out, g_norm_ffn, w_up, w_conv, b_conv, w_down)
        mod = _ada(cond, w_ada[l], b_ada[l][None]).reshape(1 + dec_batch, 6, 1, D_MODEL)
        tbl = _bias_rows(rpb[l])
        xp, kc, vc, sf, sb = _layer(xp, mod, lambda tok: 0, wts, None, gla_bt=4, ffn_seqs=2, fuse_merge_ffn=True)
        to_cache = lambda a: a.reshape(batch, NA_HEADS, NA_HD, seq).transpose(0, 3, 1, 2)
        from_cache = lambda a: a.transpose(0, 2, 3, 1).reshape(dec_batch, NA_W, past)
        new_k.append(to_cache(kc))
        new_v.append(to_cache(vc))
        new_sf.append(sf)
        new_sb.append(sb)
        ctx = (from_cache(cache_na_k[:, l]), from_cache(cache_na_v[:, l]),
               state_gla_fwd[:, l], state_gla_bwd[:, l], tbl)
        xs, _, _, _, _ = _layer(xs, mod, lambda tok: 1 + tok // dec_seq, wts, ctx, gla_bt=dec_batch,
                               ffn_seqs=1, fuse_merge_ffn=False)
    return (xp, xs, jnp.stack(new_k, axis=1), jnp.stack(new_v, axis=1),
            jnp.stack(new_sf, axis=1), jnp.stack(new_sb, axis=1))
```

```python
import functools

import numpy as np
import jax
import jax.numpy as jnp
from jax import lax
from jax.experimental import pallas as pl
from jax.experimental.pallas import tpu as pltpu

F32 = jnp.float32
BF16 = jnp.bfloat16

D_MODEL = 1024
GRID_W = 64
GLA_HEADS, GLA_DK, GLA_DV = 4, 64, 128
GLA_QK = GLA_HEADS * GLA_DK
GLA_V = GLA_HEADS * GLA_DV
GLA_LOWRANK = 16
GLA_GATE_NORM = 16.0
GLA_CHUNK = 64
NA_HEADS, NA_HD = 8, 64
NA_W = NA_HEADS * NA_HD
NA_WIN_ROWS, NA_WIN_COLS = 8, 16
D_FF = 2816
EPS = 1e-6
IN_SIZES = (GLA_QK, GLA_QK, GLA_V, GLA_V, GLA_LOWRANK, GLA_LOWRANK, NA_W, NA_W, NA_W, D_MODEL, D_MODEL)

LANES = 128
ROW_TILE = 256
INPROJ_TILE = 512
FF_TILE = 512
MASK_VALUE = -1e30
VMEM_LIMIT = 56 * 1024 * 1024

NT_DIMS = (((1,), (1,)), ((), ()))
TN_DIMS = (((0,), (0,)), ((), ()))


def _dot(a, b):
    return jnp.dot(a, b, preferred_element_type=F32)


def _dot_nt(a, b):
    return lax.dot_general(a, b, NT_DIMS, preferred_element_type=F32)


def _dot_tn(a, b):
    return lax.dot_general(a, b, TN_DIMS, preferred_element_type=F32)


def _split3(a):
    t1 = a.astype(BF16)
    e1 = a - t1.astype(F32)
    t2 = e1.astype(BF16)
    t3 = (e1 - t2.astype(F32)).astype(BF16)
    return t1, t2, t3


def _params(*sem):
    return pltpu.CompilerParams(dimension_semantics=sem, vmem_limit_bytes=VMEM_LIMIT)


def _const_spec(shape):
    zeros = (0,) * len(shape)
    return pl.BlockSpec(shape, lambda *_: zeros, pipeline_mode=pl.Buffered(1))


def _ada_kernel(c_ref, w_ref, b_ref, o_ref):
    c = c_ref[...]
    s = c * jax.nn.sigmoid(c)
    o_ref[...] = _dot(s.astype(BF16), w_ref[...].astype(BF16)) + b_ref[...]


def _ada(cond, w_ada, b_ada):
    n = cond.shape[0]
    return pl.pallas_call(
        _ada_kernel,
        grid=(6,),
        in_specs=[pl.BlockSpec((n, D_MODEL), lambda j: (0, 0)),
                  pl.BlockSpec((D_MODEL, D_MODEL), lambda j: (0, j)),
                  pl.BlockSpec((1, D_MODEL), lambda j: (0, j))],
        out_specs=pl.BlockSpec((n, D_MODEL), lambda j: (0, j)),
        out_shape=jax.ShapeDtypeStruct((n, 6 * D_MODEL), F32),
        compiler_params=_params("arbitrary"),
        name="ada",
    )(cond, w_ada, b_ada)


def _rms(x):
    return x * lax.rsqrt(jnp.mean(x * x, axis=-1, keepdims=True) + EPS)


_W_OFF = [int(o) for o in np.cumsum((0,) + IN_SIZES)]
W_GLA, W_LR, W_NA, W_MERGE, W_END = _W_OFF[0], _W_OFF[4], _W_OFF[6], _W_OFF[9], _W_OFF[11]
CAST_ROWS = 736


def _cast_kernel(w_ref, o_ref):
    o_ref[...] = w_ref[...].astype(BF16)


def _cast_bf16(w, rows):
    n, c = w.shape
    assert n % rows == 0
    return pl.pallas_call(
        _cast_kernel,
        grid=(n // rows,),
        in_specs=[pl.BlockSpec((rows, c), lambda i: (i, 0))],
        out_specs=pl.BlockSpec((rows, c), lambda i: (i, 0)),
        out_shape=jax.ShapeDtypeStruct((n, c), BF16),
        compiler_params=_params("parallel"),
        name="cast_bf16",
    )(w)


def _inproj_kernel(transpose_kv, x_ref, mod_ref, g_ref, w_ref, wlr_ref, blr_ref, gqn_ref, gkn_ref, seg_ref,
                   q_ref, k_ref, v_ref, gr_ref, la_ref, nq_ref, nk_ref, nv_ref, sg_ref):
    h = _rms(x_ref[...]) * g_ref[...]
    h = (h * (1.0 + mod_ref[1]) + mod_ref[0]).astype(BF16)

    u = _dot_nt(h, w_ref[W_GLA:W_LR, :])
    q_ref[...] = (u[:, :GLA_QK] * (GLA_DK ** -0.5)).astype(BF16)
    k_ref[...] = u[:, GLA_QK:2 * GLA_QK].astype(BF16)
    v_ref[...] = u[:, 2 * GLA_QK:2 * GLA_QK + GLA_V].astype(BF16)
    gr = u[:, 2 * GLA_QK + GLA_V:]
    gr_ref[...] = (gr * jax.nn.sigmoid(gr)).astype(BF16)

    z = _dot(_dot_nt(h, w_ref[W_LR:W_LR + LANES, :]).astype(BF16), wlr_ref[...]) + blr_ref[...]
    log_sig = jnp.minimum(z, 0.0) - jnp.log1p(jnp.exp(-jnp.abs(z)))
    la_ref[...] = log_sig * (1.0 / GLA_GATE_NORM)

    u = _dot_nt(h, w_ref[W_NA:W_MERGE, :])

    def head_norm(a, gain):
        ssq = _dot((a * a).astype(BF16), seg_ref[...])
        return a * lax.rsqrt(ssq * (1.0 / NA_HD) + EPS) * gain

    nq_ref[...] = (head_norm(u[:, :NA_W], gqn_ref[...]) * (NA_HD ** -0.5)).astype(BF16)
    nk = head_norm(u[:, NA_W:2 * NA_W], gkn_ref[...])
    nv = u[:, 2 * NA_W:]
    if transpose_kv:
        seq = nk_ref.shape[2]
        for s in range(nk_ref.shape[0]):
            nk_ref[s] = nk[s * seq:(s + 1) * seq].T
            nv_ref[s] = nv[s * seq:(s + 1) * seq].T
    else:
        nk_ref[...] = nk.astype(nk_ref.dtype)
        nv_ref[...] = nv.astype(nv_ref.dtype)

    sg_ref[...] = jax.nn.sigmoid(_dot_nt(h, w_ref[W_MERGE:W_END, :])).astype(BF16)


def _inproj(x, mod, mod_row, g_mix, w_t, w_lr, b_lr, g_qn, g_kn, seg, batch, seq, transpose_kv):
    n = x.shape[0]
    tm = INPROJ_TILE
    assert (seq % tm == 0 or tm % seq == 0) and W_LR % 16 == 0 and W_NA % 16 == 0 and W_MERGE % 16 == 0
    row = lambda c: pl.BlockSpec((tm, c), lambda i: (i, 0))
    outs = [(GLA_QK, BF16), (GLA_QK, BF16), (GLA_V, BF16), (GLA_V, BF16), (2 * GLA_QK, F32), (NA_W, BF16)]
    out_specs = [row(c) for c, _ in outs]
    out_shape = [jax.ShapeDtypeStruct((n, c), dt) for c, dt in outs]
    if transpose_kv:
        if tm >= seq:
            kv_spec = pl.BlockSpec((tm // seq, NA_W, seq), lambda i: (i, 0, 0))
        else:
            tps = seq // tm
            kv_spec = pl.BlockSpec((1, NA_W, tm), lambda i: (i // tps, 0, i % tps))
        kv_shape = jax.ShapeDtypeStruct((batch, NA_W, seq), F32)
    else:
        kv_spec = row(NA_W)
        kv_shape = jax.ShapeDtypeStruct((n, NA_W), BF16)
    out_specs += [kv_spec, kv_spec, row(2 * D_MODEL)]
    out_shape += [kv_shape, kv_shape, jax.ShapeDtypeStruct((n, 2 * D_MODEL), BF16)]
    return pl.pallas_call(
        functools.partial(_inproj_kernel, transpose_kv),
        grid=(n // tm,),
        in_specs=[row(D_MODEL),
                  pl.BlockSpec((None, 6, 1, D_MODEL), lambda i: (mod_row(i * tm), 0, 0, 0)),
                  _const_spec(g_mix.shape), _const_spec(w_t.shape), _const_spec(w_lr.shape),
                  _const_spec(b_lr.shape), _const_spec(g_qn.shape), _const_spec(g_kn.shape),
                  _const_spec(seg.shape)],
        out_specs=out_specs,
        out_shape=out_shape,
        compiler_params=_params("parallel"),
        name="inproj",
    )(x, mod, g_mix, w_t, w_lr, b_lr, g_qn, g_kn, seg)


GLA_SAFE_DECAY = 80.0


def _chunk_cumsum(x, reverse):
    c = x.shape[0]
    rows = lax.broadcasted_iota(jnp.int32, (c, 1), 0)
    s = 1
    while s < c:
        if reverse:
            x = x + jnp.where(rows < c - s, pltpu.roll(x, c - s, axis=0), 0.0)
        else:
            x = x + jnp.where(rows >= s, pltpu.roll(x, s, axis=0), 0.0)
        s *= 2
    return x


def _gla_kernel(has_s0, nsteps, cps, *refs):
    qf, kf, vf, laf, qb, kb, vb, lab = refs[:8]
    if has_s0:
        s0f, s0b, of, ob, sf, sb, st, oi, q32, bcs = refs[8:]
    else:
        of, ob, sf, sb, st, oi, q32, bcs = refs[8:]
        s0f = s0b = None
    n = pl.program_id(1)
    bt = qf.shape[0]
    C, H, DK, DV = GLA_CHUNK, GLA_HEADS, GLA_DK, GLA_DV
    assert C == DK
    HC = H * C
    row = lax.broadcasted_iota(jnp.int32, (HC, HC), 0)
    col = lax.broadcasted_iota(jnp.int32, (HC, HC), 1)
    same_head = (row // C) == (col // C)
    t_in, s_in = row % C, col % C
    vis = (same_head & (t_in >= s_in), same_head & (t_in <= s_in))
    ins = ((qf, kf, vf, laf, of), (qb, kb, vb, lab, ob))
    order = (list(range(cps)), list(range(cps))[::-1])
    units = [(d, b, c) for c in range(cps) for d in range(2) for b in range(bt)]
    heads = [(slice(h * DK, (h + 1) * DK), slice(h * DV, (h + 1) * DV)) for h in range(H)]

    def chunk(d, which, b, c):
        return ins[d][which][b, c * C:(c + 1) * C, :]

    def rows4(a):
        return jnp.concatenate([a] * H, axis=0)

    def block_diag(a):
        return jnp.where(same_head, rows4(a), jnp.zeros((), a.dtype))

    def v_stack(v):
        return jnp.concatenate([v[:, vs] for _, vs in heads], axis=0)

    @pl.when(n == 0)
    def _init():
        for d, s0 in enumerate((s0f, s0b)):
            for b in range(bt):
                for h in range(H):
                    st[d, b, h * DK:(h + 1) * DK, :] = s0[b, h] if has_s0 else jnp.zeros((DK, DV), F32)

    bc, tot, q_bd, kh_bd, et = {}, {}, {}, {}, {}
    for u in units:
        d, b, c = u
        bc[u] = _chunk_cumsum(chunk(d, 3, b, c), reverse=d == 1)
        tot[u] = bc[u][C - 1:C, :] if d == 0 else bc[u][0:1, :]
    worst = functools.reduce(jnp.maximum, [jnp.max(-tot[u]) for u in units])
    safe = worst <= GLA_SAFE_DECAY

    for u in units:
        d, b, c = u
        q = chunk(d, 0, b, c).astype(F32)
        k = chunk(d, 1, b, c).astype(F32)
        q_bd[u] = block_diag((q * jnp.exp(bc[u])).astype(BF16))
        kh_bd[u] = block_diag((k * jnp.exp(tot[u] - bc[u])).astype(BF16))
        et[u] = jnp.exp(tot[u])

    @pl.when(safe)
    def _intra_factorised():
        for u in units:
            d, b, c = u
            k = chunk(d, 1, b, c).astype(F32)
            kt = rows4((k * jnp.exp(-bc[u])).astype(BF16))
            s = jnp.where(vis[d], _dot_nt(q_bd[u], kt), 0.0).astype(BF16)
            oi[d, b, c] = _dot(s, v_stack(chunk(d, 2, b, c)))

    @pl.when(jnp.logical_not(safe))
    def _intra_pairwise():
        for u in units:
            d, b, c = u
            q32[d, b, c] = chunk(d, 0, b, c).astype(F32)
            bcs[d, b, c] = bc[u]
        svis = lax.broadcasted_iota(jnp.int32, (C, 1), 0)
        for u in units:
            d, b, c = u
            k = chunk(d, 1, b, c).astype(F32)
            v = chunk(d, 2, b, c).astype(F32)
            for h, (ks, vs) in enumerate(heads):
                kh_, bh, vh = k[:, ks], bc[u][:, ks], v[:, vs]

                def one_row(t, carry, d=d, b=b, c=c, h=h, ks=ks, kh_=kh_, bh=bh, vh=vh):
                    bt_row = bcs[d, b, c, pl.ds(t, 1), :][:, ks]
                    q_row = q32[d, b, c, pl.ds(t, 1), :][:, ks]
                    seen = (svis <= t) if d == 0 else (svis >= t)
                    w = jnp.exp(jnp.where(seen, bt_row - bh, -jnp.inf)) * kh_ * q_row
                    s_col = jnp.sum(w, axis=1, keepdims=True)
                    oi[d, b, c, pl.ds(h * C + t, 1), :] = jnp.sum(s_col * vh, axis=0, keepdims=True)
                    return carry

                lax.fori_loop(0, C, one_row, 0)

    ones = jnp.ones((16, DV), BF16)
    for j in range(cps):
        for d in range(2):
            c = order[d][j]
            for b in range(bt):
                u = (d, b, c)
                o = oi[d, b, c] + _dot(q_bd[u], st[d, b].astype(BF16))
                ins[d][4][b, c * C:(c + 1) * C, :] = jnp.concatenate(
                    [o[h * C:(h + 1) * C] for h in range(H)], axis=1).astype(ins[d][4].dtype)
        for d in range(2):
            c = order[d][j]
            for b in range(bt):
                u = (d, b, c)
                e_rows = jnp.concatenate(list(_split3(et[u])) + [jnp.zeros((13, H * DK), BF16)], axis=0)
                decay = _dot_tn(e_rows, ones)
                st[d, b] = decay * st[d, b] + _dot_tn(kh_bd[u], v_stack(chunk(d, 2, b, c)))

    @pl.when(n == nsteps - 1)
    def _fin():
        for b in range(bt):
            for h in range(H):
                sf[b, h] = st[0, b, h * DK:(h + 1) * DK, :]
                sb[b, h] = st[1, b, h * DK:(h + 1) * DK, :]


def _gla(q, k, v, la, s0, batch, seq, bt, cps):
    C = GLA_CHUNK
    rows = cps * C
    assert seq % rows == 0 and batch % bt == 0
    ns = seq // rows
    q, k, v, la = (a.reshape(batch, seq, a.shape[-1]) for a in (q, k, v, la))
    fwd = lambda c, j=0: pl.BlockSpec((bt, rows, c), lambda i, n: (i, n, j))
    bwd = lambda c, j=0: pl.BlockSpec((bt, rows, c), lambda i, n: (i, ns - 1 - n, j))
    st_spec = pl.BlockSpec((bt, GLA_HEADS, GLA_DK, GLA_DV), lambda i, n: (i, 0, 0, 0))
    in_specs = [fwd(GLA_QK), fwd(GLA_QK), fwd(GLA_V), fwd(GLA_QK, 0),
                bwd(GLA_QK), bwd(GLA_QK), bwd(GLA_V), bwd(GLA_QK, 1)]
    args = [q, k, v, la, q, k, v, la]
    if s0 is not None:
        in_specs += [st_spec, st_spec]
        args += list(s0)
    st_shape = jax.ShapeDtypeStruct((batch, GLA_HEADS, GLA_DK, GLA_DV), F32)
    o_shape = jax.ShapeDtypeStruct((batch, seq, GLA_V), BF16)
    of, ob, sf, sb = pl.pallas_call(
        functools.partial(_gla_kernel, s0 is not None, ns, cps),
        grid=(batch // bt, ns),
        in_specs=in_specs,
        out_specs=[fwd(GLA_V), bwd(GLA_V), st_spec, st_spec],
        out_shape=[o_shape, o_shape, st_shape, st_shape],
        scratch_shapes=[pltpu.VMEM((2, bt, GLA_QK, GLA_DV), F32),
                        pltpu.VMEM((2, bt, cps, GLA_HEADS * C, GLA_DV), F32),
                        pltpu.VMEM((2, bt, cps, C, GLA_QK), F32),
                        pltpu.VMEM((2, bt, cps, C, GLA_QK), F32)],
        compiler_params=_params("parallel", "arbitrary"),
        name="gla",
    )(*args)
    return of.reshape(batch * seq, GLA_V), ob.reshape(batch * seq, GLA_V), sf, sb


def _softmax_pv(scores, values):
    m = functools.reduce(jnp.maximum, [jnp.max(s, axis=-1, keepdims=True) for s in scores])
    ps = [jnp.exp(s - m) for s in scores]
    l = functools.reduce(jnp.add, [jnp.sum(p, axis=-1, keepdims=True) for p in ps])
    o = functools.reduce(jnp.add, [(_dot_nt if transposed else _dot)(p.astype(BF16), v)
                                   for p, (v, transposed) in zip(ps, values)])
    return o / l


def _ctx_attn_kernel(q_ref, kt_ref, vt_ref, o_ref):
    q = q_ref[...]
    kt = kt_ref[...].astype(BF16)
    vt = vt_ref[...].astype(BF16)
    outs = []
    for h in range(NA_HEADS):
        hs = slice(h * NA_HD, (h + 1) * NA_HD)
        outs.append(_softmax_pv([_dot(q[:, hs], kt[hs, :])], [(vt[hs, :], True)]))
    o_ref[...] = jnp.concatenate(outs, axis=-1).astype(o_ref.dtype)


def _ctx_attn(q, kt, vt, batch, seq):
    spec = pl.BlockSpec((seq, NA_W), lambda b: (b, 0))
    spec_t = pl.BlockSpec((None, NA_W, seq), lambda b: (b, 0, 0))
    return pl.pallas_call(
        _ctx_attn_kernel,
        grid=(batch,),
        in_specs=[spec, spec_t, spec_t],
        out_specs=spec,
        out_shape=jax.ShapeDtypeStruct((batch * seq, NA_W), BF16),
        compiler_params=_params("parallel"),
        name="ctx_attn",
    )(q, kt, vt)


N_DR = 2 * NA_WIN_ROWS - 1
N_DC = 2 * NA_WIN_COLS - 1
NA_QROWS = 4
NA_KROWS = NA_WIN_ROWS + NA_QROWS
BIAS_BOTH = 0
BIAS_RIGHT = N_DR - 1
BIAS_LEFT = BIAS_RIGHT + NA_WIN_ROWS
BIAS_NONE = BIAS_LEFT + NA_WIN_ROWS
BIAS_ROWS = 32


def _bias_rows(rpb):
    assert 2 * GRID_W == LANES and BIAS_NONE < BIAS_ROWS
    b = jnp.pad(rpb, ((0, 0), (0, 0), (0, GRID_W - N_DC)))
    m = jnp.full((NA_HEADS, NA_WIN_ROWS, GRID_W), MASK_VALUE, F32)
    both = jnp.concatenate([b[:, :-1], b[:, 1:]], axis=-1)
    right = jnp.concatenate([m, b[:, :NA_WIN_ROWS]], axis=-1)
    left = jnp.concatenate([b[:, NA_WIN_ROWS - 1:], m], axis=-1)
    none = jnp.concatenate([m[:, :1], m[:, :1]], axis=-1)
    pad = jnp.zeros((NA_HEADS, BIAS_ROWS - BIAS_NONE - 1, LANES), F32)
    return jnp.concatenate([both, right, left, none, pad], axis=1).reshape(NA_HEADS * BIAS_ROWS, LANES)


def _na_kernel(rows, q_ref, k_ref, v_ref, kct_ref, vct_ref, bias_ref, o_ref, ctx_ref):
    g = pl.program_id(1)

    @pl.when(g == 0)
    def _cast_ctx():
        ctx_ref[0] = kct_ref[...].astype(BF16)
        ctx_ref[1] = vct_ref[...].astype(BF16)

    wr = NA_WIN_ROWS
    ws = jnp.clip(g * NA_QROWS - wr // 2, 0, rows - NA_KROWS)
    start = pl.multiple_of(ws * GRID_W, GRID_W)
    q = q_ref[...]
    kl = k_ref[pl.ds(start, NA_KROWS * GRID_W), :]
    vl = v_ref[pl.ds(start, NA_KROWS * GRID_W), :]

    tile_row = []
    for i in range(NA_QROWS):
        r = g * NA_QROWS + i
        r0 = jnp.clip(r - wr // 2, 0, rows - wr)
        for j in range(NA_KROWS // 2):
            kr = ws + 2 * j
            in0 = (kr >= r0) & (kr < r0 + wr)
            in1 = (kr + 1 >= r0) & (kr + 1 < r0 + wr)
            dr = kr - r + wr - 1
            tile_row.append(jnp.where(in0 & in1, BIAS_BOTH + dr,
                                      jnp.where(in1, BIAS_RIGHT + dr + 1,
                                                jnp.where(in0, BIAS_LEFT + dr - (wr - 1), BIAS_NONE))))

    qc = lax.broadcasted_iota(jnp.int32, (GRID_W, LANES), 0)
    kc = lax.broadcasted_iota(jnp.int32, (GRID_W, LANES), 1) % GRID_W
    c0 = jnp.clip(qc - NA_WIN_COLS // 2, 0, GRID_W - NA_WIN_COLS)
    in_win = (kc >= c0) & (kc < c0 + NA_WIN_COLS)
    shift = LANES - (NA_WIN_COLS - 1)

    def bias_tile(h, t):
        row = jnp.broadcast_to(bias_ref[pl.ds(h * BIAS_ROWS + t, 1), :], (GRID_W, LANES))
        return jnp.where(in_win, pltpu.roll(row, shift, axis=1, stride=1, stride_axis=0), MASK_VALUE)

    outs = []
    for h in range(NA_HEADS):
        hs = slice(h * NA_HD, (h + 1) * NA_HD)
        bias = jnp.concatenate(
            [jnp.concatenate([bias_tile(h, tile_row[i * (NA_KROWS // 2) + j]) for j in range(NA_KROWS // 2)], axis=1)
             for i in range(NA_QROWS)], axis=0)
        s_loc = _dot_nt(q[:, hs], kl[:, hs]) + bias
        s_ctx = _dot(q[:, hs], ctx_ref[0, hs, :])
        outs.append(_softmax_pv([s_loc, s_ctx], [(vl[:, hs], False), (ctx_ref[1, hs, :], True)]))
    o_ref[...] = jnp.concatenate(outs, axis=-1).astype(o_ref.dtype)


def _na_attn(q, k, v, kt_ctx, vt_ctx, bias_rows, batch, seq):
    rows = seq // GRID_W
    wr = NA_WIN_ROWS
    assert rows >= NA_KROWS and rows % NA_QROWS == 0 and wr % 2 == 0 and NA_KROWS % 2 == 0
    for g in range(rows // NA_QROWS):
        ws = min(max(g * NA_QROWS - wr // 2, 0), rows - NA_KROWS)
        for r in range(g * NA_QROWS, (g + 1) * NA_QROWS):
            r0 = min(max(r - wr // 2, 0), rows - wr)
            assert ws <= r0 and r0 + wr <= ws + NA_KROWS
    past = kt_ctx.shape[2]
    q, k, v = (a.reshape(batch, seq, NA_W) for a in (q, k, v))
    whole = pl.BlockSpec((None, seq, NA_W), lambda b, g: (b, 0, 0))
    whole_t = pl.BlockSpec((None, NA_W, past), lambda b, g: (b, 0, 0))
    tile = pl.BlockSpec((None, NA_QROWS * GRID_W, NA_W), lambda b, g: (b, g, 0))
    o = pl.pallas_call(
        functools.partial(_na_kernel, rows),
        grid=(batch, rows // NA_QROWS),
        in_specs=[tile, whole, whole, whole_t, whole_t, _const_spec(bias_rows.shape)],
        out_specs=tile,
        out_shape=jax.ShapeDtypeStruct((batch, seq, NA_W), BF16),
        scratch_shapes=[pltpu.VMEM((2, NA_W, past), BF16)],
        compiler_params=_params("parallel", "arbitrary"),
        name="na_attn",
    )(q, k, v, kt_ctx, vt_ctx, bias_rows)
    return o.reshape(batch * seq, NA_W)


def _merge_body(x, mod_ref, of_ref, ob_ref, gr_ref, na_ref, sg_ref, ggla_ref, wg_ref, wn_ref, wo_ref, gffn_ref):
    o = of_ref[...].astype(F32) + ob_ref[...].astype(F32)
    on = jnp.concatenate([_rms(o[:, h * GLA_DV:(h + 1) * GLA_DV]) for h in range(GLA_HEADS)], axis=-1)
    on = on * ggla_ref[...] * gr_ref[...].astype(F32)
    b_gla = _dot(on.astype(BF16), wg_ref[...])
    b_na = _dot(na_ref[...], wn_ref[...])
    sg = sg_ref[...].astype(F32)
    mix = _dot((sg[:, :D_MODEL] * b_gla + sg[:, D_MODEL:] * b_na).astype(BF16), wo_ref[...])
    x1 = x + mod_ref[2] * mix
    h2 = (_rms(x1) * gffn_ref[...] * (1.0 + mod_ref[4]) + mod_ref[3]).astype(BF16)
    return x1, h2


def _merge_kernel(x_ref, mod_ref, of_ref, ob_ref, gr_ref, na_ref, sg_ref, ggla_ref, wg_ref, wn_ref, wo_ref,
                  gffn_ref, x1_ref, h2_ref):
    x1_ref[...], h2_ref[...] = _merge_body(x_ref[...], mod_ref, of_ref, ob_ref, gr_ref, na_ref, sg_ref, ggla_ref,
                                           wg_ref, wn_ref, wo_ref, gffn_ref)


def _merge(x, mod, mod_row, of, ob, gr, na, sg, g_gla, w_gla_o, w_na_o, w_out, g_ffn):
    n = x.shape[0]
    tm = ROW_TILE
    row = lambda c: pl.BlockSpec((tm, c), lambda i: (i, 0))
    return pl.pallas_call(
        _merge_kernel,
        grid=(n // tm,),
        in_specs=[row(D_MODEL),
                  pl.BlockSpec((None, 6, 1, D_MODEL), lambda i: (mod_row(i * tm), 0, 0, 0)),
                  row(GLA_V), row(GLA_V), row(GLA_V), row(NA_W), row(2 * D_MODEL),
                  _const_spec(g_gla.shape), _const_spec(w_gla_o.shape), _const_spec(w_na_o.shape),
                  _const_spec(w_out.shape), _const_spec(g_ffn.shape)],
        out_specs=[row(D_MODEL), row(D_MODEL)],
        out_shape=[jax.ShapeDtypeStruct((n, D_MODEL), F32), jax.ShapeDtypeStruct((n, D_MODEL), BF16)],
        compiler_params=_params("parallel"),
        name="merge",
    )(x, mod, of, ob, gr, na, sg, g_gla, w_gla_o, w_na_o, w_out, g_ffn)


def _ffn_body(seq, h, wup_ref, wc_ref, bc_ref, wdn_ref, acc_ref, a_ref, g_ref):
    t = h.shape[0]
    pos = lax.broadcasted_iota(jnp.int32, (t, 1), 0) % seq
    first = pos == 0
    last = pos == seq - 1
    tiles = [(lo, min(lo + FF_TILE, D_FF)) for lo in range(0, D_FF, FF_TILE)]

    def up(j):
        lo, hi = tiles[j]
        a_ref[j % 2, :, :hi - lo] = _dot(h, wup_ref[:, lo:hi])
        g_ref[j % 2, :, :hi - lo] = _dot(h, wup_ref[:, D_FF + lo:D_FF + hi])

    def conv(u, lo, hi):
        prev = jnp.where(first, 0.0, pltpu.roll(u, 1, axis=0))
        nxt = jnp.where(last, 0.0, pltpu.roll(u, t - 1, axis=0))
        return wc_ref[0:1, lo:hi] * prev + wc_ref[1:2, lo:hi] * u + wc_ref[2:3, lo:hi] * nxt + bc_ref[:, lo:hi]

    up(0)
    for j, (lo, hi) in enumerate(tiles):
        if j + 1 < len(tiles):
            up(j + 1)
        a = conv(a_ref[j % 2, :, :hi - lo], lo, hi)
        g = conv(g_ref[j % 2, :, :hi - lo], D_FF + lo, D_FF + hi)
        part = _dot((a * (g * jax.nn.sigmoid(g))).astype(BF16), wdn_ref[lo:hi, :])
        if j == 0:
            acc_ref[...] = part
        else:
            acc_ref[...] += part


def _ffn_kernel(seq, h_ref, x1_ref, mod_ref, wup_ref, wc_ref, bc_ref, wdn_ref, o_ref, a_ref, g_ref):
    _ffn_body(seq, h_ref[...], wup_ref, wc_ref, bc_ref, wdn_ref, o_ref, a_ref, g_ref)
    o_ref[...] = x1_ref[...] + mod_ref[5] * o_ref[...]


def _merge_ffn_kernel(seq, x_ref, mod_ref, of_ref, ob_ref, gr_ref, na_ref, sg_ref, ggla_ref, wg_ref, wn_ref, wo_ref,
                      gffn_ref, wup_ref, wc_ref, bc_ref, wdn_ref, o_ref, x1_ref, a_ref, g_ref):
    x1_ref[...], h2 = _merge_body(x_ref[...], mod_ref, of_ref, ob_ref, gr_ref, na_ref, sg_ref, ggla_ref,
                                  wg_ref, wn_ref, wo_ref, gffn_ref)
    _ffn_body(seq, h2, wup_ref, wc_ref, bc_ref, wdn_ref, o_ref, a_ref, g_ref)
    o_ref[...] = x1_ref[...] + mod_ref[5] * o_ref[...]


def _merge_ffn(x, mod, mod_row, of, ob, gr, na, sg, g_gla, w_gla_o, w_na_o, w_out, g_ffn, w_up, w_conv, b_conv,
               w_down, batch, seq, seqs_per_step):
    rows = seq * seqs_per_step
    row = lambda c: pl.BlockSpec((rows, c), lambda i: (i, 0))
    consts = (g_gla, w_gla_o, w_na_o, w_out, g_ffn, w_up, w_conv, b_conv, w_down)
    return pl.pallas_call(
        functools.partial(_merge_ffn_kernel, seq),
        grid=(batch // seqs_per_step,),
        in_specs=[row(D_MODEL),
                  pl.BlockSpec((None, 6, 1, D_MODEL), lambda i: (mod_row(i * rows), 0, 0, 0)),
                  row(GLA_V), row(GLA_V), row(GLA_V), row(NA_W), row(2 * D_MODEL)]
                 + [_const_spec(c.shape) for c in consts],
        out_specs=row(D_MODEL),
        out_shape=jax.ShapeDtypeStruct((batch * seq, D_MODEL), F32),
        scratch_shapes=[pltpu.VMEM((rows, D_MODEL), F32),
                        pltpu.VMEM((2, rows, FF_TILE), F32), pltpu.VMEM((2, rows, FF_TILE), F32)],
        compiler_params=_params("parallel"),
        name="merge_ffn",
    )(x, mod, of, ob, gr, na, sg, *consts)


def _ffn(h2, x1, mod, mod_row, w_up, w_conv, b_conv, w_down, batch, seq, seqs_per_step):
    rows = seq * seqs_per_step
    blk = lambda: pl.BlockSpec((rows, D_MODEL), lambda b: (b, 0))
    return pl.pallas_call(
        functools.partial(_ffn_kernel, seq),
        grid=(batch // seqs_per_step,),
        in_specs=[blk(), blk(),
                  pl.BlockSpec((None, 6, 1, D_MODEL), lambda b: (mod_row(b * rows), 0, 0, 0)),
                  _const_spec(w_up.shape), _const_spec(w_conv.shape), _const_spec(b_conv.shape),
                  _const_spec(w_down.shape)],
        out_specs=blk(),
        out_shape=jax.ShapeDtypeStruct((batch * seq, D_MODEL), F32),
        scratch_shapes=[pltpu.VMEM((2, rows, FF_TILE), F32), pltpu.VMEM((2, rows, FF_TILE), F32)],
        compiler_params=_params("parallel"),
        name="conv_ffn",
    )(h2, x1, mod, w_up, w_conv, b_conv, w_down)


def _layer(x, mod, mod_row, wts, ctx, gla_tile, ffn_seqs, fuse_merge_ffn):
    batch, seq, _ = x.shape
    x2 = x.reshape(batch * seq, D_MODEL)
    q, k, v, gr, la, nq, nk, nv, sg = _inproj(x2, mod, mod_row, wts["g_mix"], wts["w_t"], wts["w_lr"],
                                              wts["b_lr"], wts["g_qn"], wts["g_kn"], wts["seg"], batch, seq,
                                              transpose_kv=ctx is None)
    s0 = None if ctx is None else (ctx[2], ctx[3])
    of, ob, sf, sb = _gla(q, k, v, la, s0, batch, seq, *gla_tile)
    if ctx is None:
        o_na = _ctx_attn(nq, nk, nv, batch, seq)
    else:
        o_na = _na_attn(nq, nk, nv, ctx[0], ctx[1], ctx[4], batch, seq)
    merge_w = (wts["g_gla"], wts["w_gla_o"], wts["w_na_o"], wts["w_out"], wts["g_ffn"])
    ffn_w = (wts["w_up"], wts["w_conv"], wts["b_conv"], wts["w_down"])
    if fuse_merge_ffn:
        y = _merge_ffn(x2, mod, mod_row, of, ob, gr, o_na, sg, *merge_w, *ffn_w, batch, seq, ffn_seqs)
    else:
        x1, h2 = _merge(x2, mod, mod_row, of, ob, gr, o_na, sg, *merge_w)
        y = _ffn(h2, x1, mod, mod_row, *ffn_w, batch, seq, ffn_seqs)
    return y.reshape(batch, seq, D_MODEL), nk, nv, sf, sb


def _pack_weights(l, w_in, w_alpha_fwd, b_alpha_fwd, w_alpha_bwd, b_alpha_bwd, g_norm_mix, g_gla_norm,
                  g_q_norm, g_k_norm, w_gla_o, w_na_o, w_out, g_norm_ffn, w_up, w_conv, b_conv, w_down):
    w_t = _cast_bf16(jnp.swapaxes(w_in[l], 0, 1), CAST_ROWS)
    w_lr = jnp.zeros((LANES, 2 * GLA_QK), F32)
    w_lr = w_lr.at[:GLA_LOWRANK, :GLA_QK].set(w_alpha_fwd[l])
    w_lr = w_lr.at[GLA_LOWRANK:2 * GLA_LOWRANK, GLA_QK:].set(w_alpha_bwd[l]).astype(BF16)
    seg = jnp.asarray(np.kron(np.eye(NA_HEADS), np.ones((NA_HD, NA_HD))), BF16)
    return dict(
        g_mix=g_norm_mix[l][None], w_t=w_t, w_lr=w_lr,
        b_lr=jnp.concatenate([b_alpha_fwd[l], b_alpha_bwd[l]])[None],
        g_qn=jnp.tile(g_q_norm[l], NA_HEADS)[None], g_kn=jnp.tile(g_k_norm[l], NA_HEADS)[None], seg=seg,
        g_gla=jnp.tile(g_gla_norm[l], GLA_HEADS)[None],
        w_gla_o=w_gla_o[l].astype(BF16), w_na_o=w_na_o[l].astype(BF16), w_out=w_out[l].astype(BF16),
        g_ffn=g_norm_ffn[l][None], w_up=w_up[l].astype(BF16), w_conv=w_conv[l], b_conv=b_conv[l][None],
        w_down=w_down[l].astype(BF16))


def kernel(x_prompt, x_sample, c, cache_na_k, cache_na_v, state_gla_fwd, state_gla_bwd, c_ctx, w_ada, b_ada,
           g_norm_mix, w_in, w_alpha_fwd, b_alpha_fwd, w_alpha_bwd, b_alpha_bwd, g_gla_norm, g_q_norm, g_k_norm,
           rpb, w_gla_o, w_na_o, w_out, g_norm_ffn, w_up, w_conv, b_conv, w_down):
    depth = w_in.shape[0]
    batch, seq, _ = x_prompt.shape
    dec_batch, dec_seq, _ = x_sample.shape
    past = cache_na_k.shape[2]
    cond = jnp.concatenate([c_ctx[None], c], axis=0)
    xp, xs = x_prompt, x_sample
    new_k, new_v, new_sf, new_sb = [], [], [], []
    for l in range(depth):
        wts = _pack_weights(l, w_in, w_alpha_fwd, b_alpha_fwd, w_alpha_bwd, b_alpha_bwd, g_norm_mix, g_gla_norm,
                            g_q_norm, g_k_norm, w_gla_o, w_na_o, w_out, g_norm_ffn, w_up, w_conv, b_conv, w_down)
        mod = _ada(cond, w_ada[l], b_ada[l][None]).reshape(1 + dec_batch, 6, 1, D_MODEL)
        tbl = _bias_rows(rpb[l])
        xp, kc, vc, sf, sb = _layer(xp, mod, lambda tok: 0, wts, None, gla_tile=(4, 2), ffn_seqs=2,
                                    fuse_merge_ffn=True)
        to_cache = lambda a: a.reshape(batch, NA_HEADS, NA_HD, seq).transpose(0, 3, 1, 2)
        from_cache = lambda a: a.transpose(0, 2, 3, 1).reshape(dec_batch, NA_W, past)
        new_k.append(to_cache(kc))
        new_v.append(to_cache(vc))
        new_sf.append(sf)
        new_sb.append(sb)
        ctx = (from_cache(cache_na_k[:, l]), from_cache(cache_na_v[:, l]),
               state_gla_fwd[:, l], state_gla_bwd[:, l], tbl)
        xs, _, _, _, _ = _layer(xs, mod, lambda tok: 1 + tok // dec_seq, wts, ctx, gla_tile=(dec_batch, 2),
                               ffn_seqs=1, fuse_merge_ffn=False)
    return (xp, xs, jnp.stack(new_k, axis=1), jnp.stack(new_v, axis=1),
            jnp.stack(new_sf, axis=1), jnp.stack(new_sb, axis=1))
```

```python
import functools

import numpy as np
import jax
import jax.numpy as jnp
from jax import lax
from jax.experimental import pallas as pl
from jax.experimental.pallas import tpu as pltpu

F32 = jnp.float32
BF16 = jnp.bfloat16

D_MODEL = 1024
GRID_W = 64
GLA_HEADS, GLA_DK, GLA_DV = 4, 64, 128
GLA_QK = GLA_HEADS * GLA_DK
GLA_V = GLA_HEADS * GLA_DV
GLA_LOWRANK = 16
GLA_GATE_NORM = 16.0
GLA_CHUNK = 64
NA_HEADS, NA_HD = 8, 64
NA_W = NA_HEADS * NA_HD
NA_WIN_ROWS, NA_WIN_COLS = 8, 16
D_FF = 2816
EPS = 1e-6
IN_SIZES = (GLA_QK, GLA_QK, GLA_V, GLA_V, GLA_LOWRANK, GLA_LOWRANK, NA_W, NA_W, NA_W, D_MODEL, D_MODEL)

LANES = 128
ROW_TILE = 512
INPROJ_TILE = 512
FF_TILE = 1024
MASK_VALUE = -1e30
VMEM_LIMIT = 56 * 1024 * 1024

NT_DIMS = (((1,), (1,)), ((), ()))
TN_DIMS = (((0,), (0,)), ((), ()))


def _dot(a, b):
    return jnp.dot(a, b, preferred_element_type=F32)


def _dot_nt(a, b):
    return lax.dot_general(a, b, NT_DIMS, preferred_element_type=F32)


def _dot_tn(a, b):
    return lax.dot_general(a, b, TN_DIMS, preferred_element_type=F32)


def _split3(a):
    t1 = a.astype(BF16)
    e1 = a - t1.astype(F32)
    t2 = e1.astype(BF16)
    t3 = (e1 - t2.astype(F32)).astype(BF16)
    return t1, t2, t3


def _params(*sem):
    return pltpu.CompilerParams(dimension_semantics=sem, vmem_limit_bytes=VMEM_LIMIT)


def _const_spec(shape):
    zeros = (0,) * len(shape)
    return pl.BlockSpec(shape, lambda *_: zeros, pipeline_mode=pl.Buffered(1))


def _ada_kernel(c_ref, w_ref, b_ref, o_ref):
    c = c_ref[...]
    s = c * jax.nn.sigmoid(c)
    o_ref[...] = _dot(s.astype(BF16), w_ref[...].astype(BF16)) + b_ref[...]


def _ada(cond, w_ada, b_ada):
    n = cond.shape[0]
    return pl.pallas_call(
        _ada_kernel,
        grid=(6,),
        in_specs=[pl.BlockSpec((n, D_MODEL), lambda j: (0, 0)),
                  pl.BlockSpec((D_MODEL, D_MODEL), lambda j: (0, j)),
                  pl.BlockSpec((1, D_MODEL), lambda j: (0, j))],
        out_specs=pl.BlockSpec((n, D_MODEL), lambda j: (0, j)),
        out_shape=jax.ShapeDtypeStruct((n, 6 * D_MODEL), F32),
        compiler_params=_params("arbitrary"),
        name="ada",
    )(cond, w_ada, b_ada)


def _rms(x):
    return x * lax.rsqrt(jnp.mean(x * x, axis=-1, keepdims=True) + EPS)


_W_OFF = [int(o) for o in np.cumsum((0,) + IN_SIZES)]
W_GLA, W_LR, W_NA, W_MERGE, W_END = _W_OFF[0], _W_OFF[4], _W_OFF[6], _W_OFF[9], _W_OFF[11]
CAST_ROWS = 736


def _cast_kernel(w_ref, o_ref):
    o_ref[...] = w_ref[...].astype(BF16)


def _cast_bf16(w, rows):
    n, c = w.shape
    assert n % rows == 0
    return pl.pallas_call(
        _cast_kernel,
        grid=(n // rows,),
        in_specs=[pl.BlockSpec((rows, c), lambda i: (i, 0))],
        out_specs=pl.BlockSpec((rows, c), lambda i: (i, 0)),
        out_shape=jax.ShapeDtypeStruct((n, c), BF16),
        compiler_params=_params("parallel"),
        name="cast_bf16",
    )(w)


def _inproj_kernel(transpose_kv, x_ref, mod_ref, g_ref, w_ref, wlr_ref, blr_ref, gqn_ref, gkn_ref, seg_ref,
                   q_ref, k_ref, v_ref, gr_ref, la_ref, nq_ref, nk_ref, nv_ref, sg_ref):
    h = _rms(x_ref[...]) * g_ref[...]
    h = (h * (1.0 + mod_ref[1]) + mod_ref[0]).astype(BF16)

    u = _dot_nt(h, w_ref[W_GLA:W_LR, :])
    q_ref[...] = (u[:, :GLA_QK] * (GLA_DK ** -0.5)).astype(BF16)
    k_ref[...] = u[:, GLA_QK:2 * GLA_QK].astype(BF16)
    v_ref[...] = u[:, 2 * GLA_QK:2 * GLA_QK + GLA_V].astype(BF16)
    gr = u[:, 2 * GLA_QK + GLA_V:]
    gr_ref[...] = (gr * jax.nn.sigmoid(gr)).astype(BF16)

    z = _dot(_dot_nt(h, w_ref[W_LR:W_LR + LANES, :]).astype(BF16), wlr_ref[...]) + blr_ref[...]
    log_sig = jnp.minimum(z, 0.0) - jnp.log1p(jnp.exp(-jnp.abs(z)))
    la_ref[...] = log_sig * (1.0 / GLA_GATE_NORM)

    u = _dot_nt(h, w_ref[W_NA:W_MERGE, :])

    def head_norm(a, gain):
        ssq = _dot((a * a).astype(BF16), seg_ref[...])
        return a * lax.rsqrt(ssq * (1.0 / NA_HD) + EPS) * gain

    nq_ref[...] = (head_norm(u[:, :NA_W], gqn_ref[...]) * (NA_HD ** -0.5)).astype(BF16)
    nk = head_norm(u[:, NA_W:2 * NA_W], gkn_ref[...])
    nv = u[:, 2 * NA_W:]
    if transpose_kv:
        seq = nk_ref.shape[2]
        for s in range(nk_ref.shape[0]):
            nk_ref[s] = nk[s * seq:(s + 1) * seq].T
            nv_ref[s] = nv[s * seq:(s + 1) * seq].T
    else:
        nk_ref[...] = nk.astype(nk_ref.dtype)
        nv_ref[...] = nv.astype(nv_ref.dtype)

    sg_ref[...] = jax.nn.sigmoid(_dot_nt(h, w_ref[W_MERGE:W_END, :])).astype(BF16)


def _inproj(x, mod, mod_row, g_mix, w_t, w_lr, b_lr, g_qn, g_kn, seg, batch, seq, transpose_kv):
    n = x.shape[0]
    tm = INPROJ_TILE
    assert (seq % tm == 0 or tm % seq == 0) and W_LR % 16 == 0 and W_NA % 16 == 0 and W_MERGE % 16 == 0
    row = lambda c: pl.BlockSpec((tm, c), lambda i: (i, 0))
    outs = [(GLA_QK, BF16), (GLA_QK, BF16), (GLA_V, BF16), (GLA_V, BF16), (2 * GLA_QK, F32), (NA_W, BF16)]
    out_specs = [row(c) for c, _ in outs]
    out_shape = [jax.ShapeDtypeStruct((n, c), dt) for c, dt in outs]
    if transpose_kv:
        if tm >= seq:
            kv_spec = pl.BlockSpec((tm // seq, NA_W, seq), lambda i: (i, 0, 0))
        else:
            tps = seq // tm
            kv_spec = pl.BlockSpec((1, NA_W, tm), lambda i: (i // tps, 0, i % tps))
        kv_shape = jax.ShapeDtypeStruct((batch, NA_W, seq), F32)
    else:
        kv_spec = row(NA_W)
        kv_shape = jax.ShapeDtypeStruct((n, NA_W), BF16)
    out_specs += [kv_spec, kv_spec, row(2 * D_MODEL)]
    out_shape += [kv_shape, kv_shape, jax.ShapeDtypeStruct((n, 2 * D_MODEL), BF16)]
    return pl.pallas_call(
        functools.partial(_inproj_kernel, transpose_kv),
        grid=(n // tm,),
        in_specs=[row(D_MODEL),
                  pl.BlockSpec((None, 6, 1, D_MODEL), lambda i: (mod_row(i * tm), 0, 0, 0)),
                  _const_spec(g_mix.shape), _const_spec(w_t.shape), _const_spec(w_lr.shape),
                  _const_spec(b_lr.shape), _const_spec(g_qn.shape), _const_spec(g_kn.shape),
                  _const_spec(seg.shape)],
        out_specs=out_specs,
        out_shape=out_shape,
        compiler_params=_params("parallel"),
        name="inproj",
    )(x, mod, g_mix, w_t, w_lr, b_lr, g_qn, g_kn, seg)


GLA_SAFE_DECAY = 80.0


def _chunk_cumsum(x, reverse):
    c = x.shape[0]
    rows = lax.broadcasted_iota(jnp.int32, (c, 1), 0)
    s = 1
    while s < c:
        if reverse:
            x = x + jnp.where(rows < c - s, pltpu.roll(x, c - s, axis=0), 0.0)
        else:
            x = x + jnp.where(rows >= s, pltpu.roll(x, s, axis=0), 0.0)
        s *= 2
    return x


def _gla_kernel(has_s0, nsteps, cps, *refs):
    qf, kf, vf, laf, qb, kb, vb, lab = refs[:8]
    if has_s0:
        s0f, s0b, of, ob, sf, sb, st, oi, q32, bcs = refs[8:]
    else:
        of, ob, sf, sb, st, oi, q32, bcs = refs[8:]
        s0f = s0b = None
    n = pl.program_id(1)
    bt = qf.shape[0]
    C, H, DK, DV = GLA_CHUNK, GLA_HEADS, GLA_DK, GLA_DV
    assert C == DK
    HC = H * C
    row = lax.broadcasted_iota(jnp.int32, (HC, HC), 0)
    col = lax.broadcasted_iota(jnp.int32, (HC, HC), 1)
    same_head = (row // C) == (col // C)
    t_in, s_in = row % C, col % C
    vis = (same_head & (t_in >= s_in), same_head & (t_in <= s_in))
    ins = ((qf, kf, vf, laf, of), (qb, kb, vb, lab, ob))
    order = (list(range(cps)), list(range(cps))[::-1])
    units = [(d, b, c) for c in range(cps) for d in range(2) for b in range(bt)]
    heads = [(slice(h * DK, (h + 1) * DK), slice(h * DV, (h + 1) * DV)) for h in range(H)]

    def chunk(d, which, b, c):
        return ins[d][which][b, c * C:(c + 1) * C, :]

    def rows4(a):
        return jnp.concatenate([a] * H, axis=0)

    def block_diag(a):
        return jnp.where(same_head, rows4(a), jnp.zeros((), a.dtype))

    def v_stack(v):
        return jnp.concatenate([v[:, vs] for _, vs in heads], axis=0)

    @pl.when(n == 0)
    def _init():
        for d, s0 in enumerate((s0f, s0b)):
            for b in range(bt):
                for h in range(H):
                    st[d, b, h * DK:(h + 1) * DK, :] = s0[b, h] if has_s0 else jnp.zeros((DK, DV), F32)

    bc, tot, q_bd, kh_bd, et = {}, {}, {}, {}, {}
    for u in units:
        d, b, c = u
        bc[u] = _chunk_cumsum(chunk(d, 3, b, c), reverse=d == 1)
        tot[u] = bc[u][C - 1:C, :] if d == 0 else bc[u][0:1, :]
    worst = functools.reduce(jnp.maximum, [jnp.max(-tot[u]) for u in units])
    safe = worst <= GLA_SAFE_DECAY

    for u in units:
        d, b, c = u
        q = chunk(d, 0, b, c).astype(F32)
        k = chunk(d, 1, b, c).astype(F32)
        q_bd[u] = block_diag((q * jnp.exp(bc[u])).astype(BF16))
        kh_bd[u] = block_diag((k * jnp.exp(tot[u] - bc[u])).astype(BF16))
        et[u] = jnp.exp(tot[u])

    @pl.when(safe)
    def _intra_factorised():
        for u in units:
            d, b, c = u
            k = chunk(d, 1, b, c).astype(F32)
            kt = rows4((k * jnp.exp(-bc[u])).astype(BF16))
            s = jnp.where(vis[d], _dot_nt(q_bd[u], kt), 0.0).astype(BF16)
            oi[d, b, c] = _dot(s, v_stack(chunk(d, 2, b, c)))

    @pl.when(jnp.logical_not(safe))
    def _intra_pairwise():
        for u in units:
            d, b, c = u
            q32[d, b, c] = chunk(d, 0, b, c).astype(F32)
            bcs[d, b, c] = bc[u]
        svis = lax.broadcasted_iota(jnp.int32, (C, 1), 0)
        for u in units:
            d, b, c = u
            k = chunk(d, 1, b, c).astype(F32)
            v = chunk(d, 2, b, c).astype(F32)
            for h, (ks, vs) in enumerate(heads):
                kh_, bh, vh = k[:, ks], bc[u][:, ks], v[:, vs]

                def one_row(t, carry, d=d, b=b, c=c, h=h, ks=ks, kh_=kh_, bh=bh, vh=vh):
                    bt_row = bcs[d, b, c, pl.ds(t, 1), :][:, ks]
                    q_row = q32[d, b, c, pl.ds(t, 1), :][:, ks]
                    seen = (svis <= t) if d == 0 else (svis >= t)
                    w = jnp.exp(jnp.where(seen, bt_row - bh, -jnp.inf)) * kh_ * q_row
                    s_col = jnp.sum(w, axis=1, keepdims=True)
                    oi[d, b, c, pl.ds(h * C + t, 1), :] = jnp.sum(s_col * vh, axis=0, keepdims=True)
                    return carry

                lax.fori_loop(0, C, one_row, 0)

    ones = jnp.ones((16, DV), BF16)
    for j in range(cps):
        for d in range(2):
            c = order[d][j]
            for b in range(bt):
                u = (d, b, c)
                o = oi[d, b, c] + _dot(q_bd[u], st[d, b].astype(BF16))
                ins[d][4][b, c * C:(c + 1) * C, :] = jnp.concatenate(
                    [o[h * C:(h + 1) * C] for h in range(H)], axis=1).astype(ins[d][4].dtype)
        for d in range(2):
            c = order[d][j]
            for b in range(bt):
                u = (d, b, c)
                e_rows = jnp.concatenate(list(_split3(et[u])) + [jnp.zeros((13, H * DK), BF16)], axis=0)
                decay = _dot_tn(e_rows, ones)
                st[d, b] = decay * st[d, b] + _dot_tn(kh_bd[u], v_stack(chunk(d, 2, b, c)))

    @pl.when(n == nsteps - 1)
    def _fin():
        for b in range(bt):
            for h in range(H):
                sf[b, h] = st[0, b, h * DK:(h + 1) * DK, :]
                sb[b, h] = st[1, b, h * DK:(h + 1) * DK, :]


def _gla(q, k, v, la, s0, batch, seq, bt, cps):
    C = GLA_CHUNK
    rows = cps * C
    assert seq % rows == 0 and batch % bt == 0
    ns = seq // rows
    q, k, v, la = (a.reshape(batch, seq, a.shape[-1]) for a in (q, k, v, la))
    fwd = lambda c, j=0: pl.BlockSpec((bt, rows, c), lambda i, n: (i, n, j))
    bwd = lambda c, j=0: pl.BlockSpec((bt, rows, c), lambda i, n: (i, ns - 1 - n, j))
    st_spec = pl.BlockSpec((bt, GLA_HEADS, GLA_DK, GLA_DV), lambda i, n: (i, 0, 0, 0))
    in_specs = [fwd(GLA_QK), fwd(GLA_QK), fwd(GLA_V), fwd(GLA_QK, 0),
                bwd(GLA_QK), bwd(GLA_QK), bwd(GLA_V), bwd(GLA_QK, 1)]
    args = [q, k, v, la, q, k, v, la]
    if s0 is not None:
        in_specs += [st_spec, st_spec]
        args += list(s0)
    st_shape = jax.ShapeDtypeStruct((batch, GLA_HEADS, GLA_DK, GLA_DV), F32)
    o_shape = jax.ShapeDtypeStruct((batch, seq, GLA_V), BF16)
    of, ob, sf, sb = pl.pallas_call(
        functools.partial(_gla_kernel, s0 is not None, ns, cps),
        grid=(batch // bt, ns),
        in_specs=in_specs,
        out_specs=[fwd(GLA_V), bwd(GLA_V), st_spec, st_spec],
        out_shape=[o_shape, o_shape, st_shape, st_shape],
        scratch_shapes=[pltpu.VMEM((2, bt, GLA_QK, GLA_DV), F32),
                        pltpu.VMEM((2, bt, cps, GLA_HEADS * C, GLA_DV), F32),
                        pltpu.VMEM((2, bt, cps, C, GLA_QK), F32),
                        pltpu.VMEM((2, bt, cps, C, GLA_QK), F32)],
        compiler_params=_params("parallel", "arbitrary"),
        name="gla",
    )(*args)
    return of.reshape(batch * seq, GLA_V), ob.reshape(batch * seq, GLA_V), sf, sb


def _softmax_pv(scores, values):
    m = functools.reduce(jnp.maximum, [jnp.max(s, axis=-1, keepdims=True) for s in scores])
    ps = [jnp.exp(s - m) for s in scores]
    l = functools.reduce(jnp.add, [jnp.sum(p, axis=-1, keepdims=True) for p in ps])
    o = functools.reduce(jnp.add, [(_dot_nt if transposed else _dot)(p.astype(BF16), v)
                                   for p, (v, transposed) in zip(ps, values)])
    return o / l


def _ctx_attn_kernel(q_ref, kt_ref, vt_ref, o_ref):
    q = q_ref[...]
    kt = kt_ref[...].astype(BF16)
    vt = vt_ref[...].astype(BF16)
    outs = []
    for h in range(NA_HEADS):
        hs = slice(h * NA_HD, (h + 1) * NA_HD)
        outs.append(_softmax_pv([_dot(q[:, hs], kt[hs, :])], [(vt[hs, :], True)]))
    o_ref[...] = jnp.concatenate(outs, axis=-1).astype(o_ref.dtype)


def _ctx_attn(q, kt, vt, batch, seq):
    spec = pl.BlockSpec((seq, NA_W), lambda b: (b, 0))
    spec_t = pl.BlockSpec((None, NA_W, seq), lambda b: (b, 0, 0))
    return pl.pallas_call(
        _ctx_attn_kernel,
        grid=(batch,),
        in_specs=[spec, spec_t, spec_t],
        out_specs=spec,
        out_shape=jax.ShapeDtypeStruct((batch * seq, NA_W), BF16),
        compiler_params=_params("parallel"),
        name="ctx_attn",
    )(q, kt, vt)


N_DR = 2 * NA_WIN_ROWS - 1
N_DC = 2 * NA_WIN_COLS - 1
NA_QROWS = 4
NA_KROWS = NA_WIN_ROWS + NA_QROWS
BIAS_BOTH = 0
BIAS_RIGHT = N_DR - 1
BIAS_LEFT = BIAS_RIGHT + NA_WIN_ROWS
BIAS_NONE = BIAS_LEFT + NA_WIN_ROWS
BIAS_ROWS = 32


def _bias_rows(rpb):
    assert 2 * GRID_W == LANES and BIAS_NONE < BIAS_ROWS
    b = jnp.pad(rpb, ((0, 0), (0, 0), (0, GRID_W - N_DC)))
    m = jnp.full((NA_HEADS, NA_WIN_ROWS, GRID_W), MASK_VALUE, F32)
    both = jnp.concatenate([b[:, :-1], b[:, 1:]], axis=-1)
    right = jnp.concatenate([m, b[:, :NA_WIN_ROWS]], axis=-1)
    left = jnp.concatenate([b[:, NA_WIN_ROWS - 1:], m], axis=-1)
    none = jnp.concatenate([m[:, :1], m[:, :1]], axis=-1)
    pad = jnp.zeros((NA_HEADS, BIAS_ROWS - BIAS_NONE - 1, LANES), F32)
    return jnp.concatenate([both, right, left, none, pad], axis=1).reshape(NA_HEADS * BIAS_ROWS, LANES)


def _na_kernel(rows, q_ref, k_ref, v_ref, kct_ref, vct_ref, bias_ref, o_ref, ctx_ref):
    g = pl.program_id(1)

    @pl.when(g == 0)
    def _cast_ctx():
        ctx_ref[0] = kct_ref[...].astype(BF16)
        ctx_ref[1] = vct_ref[...].astype(BF16)

    wr = NA_WIN_ROWS
    ws = jnp.clip(g * NA_QROWS - wr // 2, 0, rows - NA_KROWS)
    start = pl.multiple_of(ws * GRID_W, GRID_W)
    q = q_ref[...]
    kl = k_ref[pl.ds(start, NA_KROWS * GRID_W), :]
    vl = v_ref[pl.ds(start, NA_KROWS * GRID_W), :]

    tile_row = []
    for i in range(NA_QROWS):
        r = g * NA_QROWS + i
        r0 = jnp.clip(r - wr // 2, 0, rows - wr)
        for j in range(NA_KROWS // 2):
            kr = ws + 2 * j
            in0 = (kr >= r0) & (kr < r0 + wr)
            in1 = (kr + 1 >= r0) & (kr + 1 < r0 + wr)
            dr = kr - r + wr - 1
            tile_row.append(jnp.where(in0 & in1, BIAS_BOTH + dr,
                                      jnp.where(in1, BIAS_RIGHT + dr + 1,
                                                jnp.where(in0, BIAS_LEFT + dr - (wr - 1), BIAS_NONE))))

    qc = lax.broadcasted_iota(jnp.int32, (GRID_W, LANES), 0)
    kc = lax.broadcasted_iota(jnp.int32, (GRID_W, LANES), 1) % GRID_W
    c0 = jnp.clip(qc - NA_WIN_COLS // 2, 0, GRID_W - NA_WIN_COLS)
    in_win = (kc >= c0) & (kc < c0 + NA_WIN_COLS)
    shift = LANES - (NA_WIN_COLS - 1)

    def bias_tile(h, t):
        row = jnp.broadcast_to(bias_ref[pl.ds(h * BIAS_ROWS + t, 1), :], (GRID_W, LANES))
        return jnp.where(in_win, pltpu.roll(row, shift, axis=1, stride=1, stride_axis=0), MASK_VALUE)

    outs = []
    for h in range(NA_HEADS):
        hs = slice(h * NA_HD, (h + 1) * NA_HD)
        bias = jnp.concatenate(
            [jnp.concatenate([bias_tile(h, tile_row[i * (NA_KROWS // 2) + j]) for j in range(NA_KROWS // 2)], axis=1)
             for i in range(NA_QROWS)], axis=0)
        s_loc = _dot_nt(q[:, hs], kl[:, hs]) + bias
        s_ctx = _dot(q[:, hs], ctx_ref[0, hs, :])
        outs.append(_softmax_pv([s_loc, s_ctx], [(vl[:, hs], False), (ctx_ref[1, hs, :], True)]))
    o_ref[...] = jnp.concatenate(outs, axis=-1).astype(o_ref.dtype)


def _na_attn(q, k, v, kt_ctx, vt_ctx, bias_rows, batch, seq):
    rows = seq // GRID_W
    wr = NA_WIN_ROWS
    assert rows >= NA_KROWS and rows % NA_QROWS == 0 and wr % 2 == 0 and NA_KROWS % 2 == 0
    for g in range(rows // NA_QROWS):
        ws = min(max(g * NA_QROWS - wr // 2, 0), rows - NA_KROWS)
        for r in range(g * NA_QROWS, (g + 1) * NA_QROWS):
            r0 = min(max(r - wr // 2, 0), rows - wr)
            assert ws <= r0 and r0 + wr <= ws + NA_KROWS
    past = kt_ctx.shape[2]
    q, k, v = (a.reshape(batch, seq, NA_W) for a in (q, k, v))
    whole = pl.BlockSpec((None, seq, NA_W), lambda b, g: (b, 0, 0))
    whole_t = pl.BlockSpec((None, NA_W, past), lambda b, g: (b, 0, 0))
    tile = pl.BlockSpec((None, NA_QROWS * GRID_W, NA_W), lambda b, g: (b, g, 0))
    o = pl.pallas_call(
        functools.partial(_na_kernel, rows),
        grid=(batch, rows // NA_QROWS),
        in_specs=[tile, whole, whole, whole_t, whole_t, _const_spec(bias_rows.shape)],
        out_specs=tile,
        out_shape=jax.ShapeDtypeStruct((batch, seq, NA_W), BF16),
        scratch_shapes=[pltpu.VMEM((2, NA_W, past), BF16)],
        compiler_params=_params("parallel", "arbitrary"),
        name="na_attn",
    )(q, k, v, kt_ctx, vt_ctx, bias_rows)
    return o.reshape(batch * seq, NA_W)


def _merge_body(x, mod_ref, of_ref, ob_ref, gr_ref, na_ref, sg_ref, ggla_ref, wg_ref, wn_ref, wo_ref, gffn_ref):
    o = of_ref[...].astype(F32) + ob_ref[...].astype(F32)
    on = jnp.concatenate([_rms(o[:, h * GLA_DV:(h + 1) * GLA_DV]) for h in range(GLA_HEADS)], axis=-1)
    on = on * ggla_ref[...] * gr_ref[...].astype(F32)
    b_gla = _dot(on.astype(BF16), wg_ref[...])
    b_na = _dot(na_ref[...], wn_ref[...])
    sg = sg_ref[...].astype(F32)
    mix = _dot((sg[:, :D_MODEL] * b_gla + sg[:, D_MODEL:] * b_na).astype(BF16), wo_ref[...])
    x1 = x + mod_ref[2] * mix
    h2 = (_rms(x1) * gffn_ref[...] * (1.0 + mod_ref[4]) + mod_ref[3]).astype(BF16)
    return x1, h2


def _merge_kernel(x_ref, mod_ref, of_ref, ob_ref, gr_ref, na_ref, sg_ref, ggla_ref, wg_ref, wn_ref, wo_ref,
                  gffn_ref, x1_ref, h2_ref):
    x1_ref[...], h2_ref[...] = _merge_body(x_ref[...], mod_ref, of_ref, ob_ref, gr_ref, na_ref, sg_ref, ggla_ref,
                                           wg_ref, wn_ref, wo_ref, gffn_ref)


def _merge(x, mod, mod_row, of, ob, gr, na, sg, g_gla, w_gla_o, w_na_o, w_out, g_ffn):
    n = x.shape[0]
    tm = ROW_TILE
    row = lambda c: pl.BlockSpec((tm, c), lambda i: (i, 0))
    return pl.pallas_call(
        _merge_kernel,
        grid=(n // tm,),
        in_specs=[row(D_MODEL),
                  pl.BlockSpec((None, 6, 1, D_MODEL), lambda i: (mod_row(i * tm), 0, 0, 0)),
                  row(GLA_V), row(GLA_V), row(GLA_V), row(NA_W), row(2 * D_MODEL),
                  _const_spec(g_gla.shape), _const_spec(w_gla_o.shape), _const_spec(w_na_o.shape),
                  _const_spec(w_out.shape), _const_spec(g_ffn.shape)],
        out_specs=[row(D_MODEL), row(D_MODEL)],
        out_shape=[jax.ShapeDtypeStruct((n, D_MODEL), F32), jax.ShapeDtypeStruct((n, D_MODEL), BF16)],
        compiler_params=_params("parallel"),
        name="merge",
    )(x, mod, of, ob, gr, na, sg, g_gla, w_gla_o, w_na_o, w_out, g_ffn)


def _ffn_body(seq, h, wup_ref, wc_ref, bc_ref, wdn_ref, acc_ref, a_ref, g_ref):
    t = h.shape[0]
    pos = lax.broadcasted_iota(jnp.int32, (t, 1), 0) % seq
    first = pos == 0
    last = pos == seq - 1
    tiles = [(lo, min(lo + FF_TILE, D_FF)) for lo in range(0, D_FF, FF_TILE)]

    def up(j):
        lo, hi = tiles[j]
        a_ref[j % 2, :, :hi - lo] = _dot(h, wup_ref[:, lo:hi])
        g_ref[j % 2, :, :hi - lo] = _dot(h, wup_ref[:, D_FF + lo:D_FF + hi])

    def conv(u, lo, hi):
        prev = jnp.where(first, 0.0, pltpu.roll(u, 1, axis=0))
        nxt = jnp.where(last, 0.0, pltpu.roll(u, t - 1, axis=0))
        return wc_ref[0:1, lo:hi] * prev + wc_ref[1:2, lo:hi] * u + wc_ref[2:3, lo:hi] * nxt + bc_ref[:, lo:hi]

    up(0)
    for j, (lo, hi) in enumerate(tiles):
        if j + 1 < len(tiles):
            up(j + 1)
        a = conv(a_ref[j % 2, :, :hi - lo], lo, hi)
        g = conv(g_ref[j % 2, :, :hi - lo], D_FF + lo, D_FF + hi)
        part = _dot((a * (g * jax.nn.sigmoid(g))).astype(BF16), wdn_ref[lo:hi, :])
        if j == 0:
            acc_ref[...] = part
        else:
            acc_ref[...] += part


def _ffn_kernel(seq, h_ref, x1_ref, mod_ref, wup_ref, wc_ref, bc_ref, wdn_ref, o_ref, a_ref, g_ref):
    _ffn_body(seq, h_ref[...], wup_ref, wc_ref, bc_ref, wdn_ref, o_ref, a_ref, g_ref)
    o_ref[...] = x1_ref[...] + mod_ref[5] * o_ref[...]


def _merge_ffn_kernel(seq, x_ref, mod_ref, of_ref, ob_ref, gr_ref, na_ref, sg_ref, ggla_ref, wg_ref, wn_ref, wo_ref,
                      gffn_ref, wup_ref, wc_ref, bc_ref, wdn_ref, o_ref, x1_ref, a_ref, g_ref):
    x1_ref[...], h2 = _merge_body(x_ref[...], mod_ref, of_ref, ob_ref, gr_ref, na_ref, sg_ref, ggla_ref,
                                  wg_ref, wn_ref, wo_ref, gffn_ref)
    _ffn_body(seq, h2, wup_ref, wc_ref, bc_ref, wdn_ref, o_ref, a_ref, g_ref)
    o_ref[...] = x1_ref[...] + mod_ref[5] * o_ref[...]


def _merge_ffn(x, mod, mod_row, of, ob, gr, na, sg, g_gla, w_gla_o, w_na_o, w_out, g_ffn, w_up, w_conv, b_conv,
               w_down, batch, seq, seqs_per_step):
    rows = seq * seqs_per_step
    row = lambda c: pl.BlockSpec((rows, c), lambda i: (i, 0))
    consts = (g_gla, w_gla_o, w_na_o, w_out, g_ffn, w_up, w_conv, b_conv, w_down)
    return pl.pallas_call(
        functools.partial(_merge_ffn_kernel, seq),
        grid=(batch // seqs_per_step,),
        in_specs=[row(D_MODEL),
                  pl.BlockSpec((None, 6, 1, D_MODEL), lambda i: (mod_row(i * rows), 0, 0, 0)),
                  row(GLA_V), row(GLA_V), row(GLA_V), row(NA_W), row(2 * D_MODEL)]
                 + [_const_spec(c.shape) for c in consts],
        out_specs=row(D_MODEL),
        out_shape=jax.ShapeDtypeStruct((batch * seq, D_MODEL), F32),
        scratch_shapes=[pltpu.VMEM((rows, D_MODEL), F32),
                        pltpu.VMEM((2, rows, FF_TILE), F32), pltpu.VMEM((2, rows, FF_TILE), F32)],
        compiler_params=_params("parallel"),
        name="merge_ffn",
    )(x, mod, of, ob, gr, na, sg, *consts)


def _ffn(h2, x1, mod, mod_row, w_up, w_conv, b_conv, w_down, batch, seq, seqs_per_step):
    rows = seq * seqs_per_step
    blk = lambda: pl.BlockSpec((rows, D_MODEL), lambda b: (b, 0))
    return pl.pallas_call(
        functools.partial(_ffn_kernel, seq),
        grid=(batch // seqs_per_step,),
        in_specs=[blk(), blk(),
                  pl.BlockSpec((None, 6, 1, D_MODEL), lambda b: (mod_row(b * rows), 0, 0, 0)),
                  _const_spec(w_up.shape), _const_spec(w_conv.shape), _const_spec(b_conv.shape),
                  _const_spec(w_down.shape)],
        out_specs=blk(),
        out_shape=jax.ShapeDtypeStruct((batch * seq, D_MODEL), F32),
        scratch_shapes=[pltpu.VMEM((2, rows, FF_TILE), F32), pltpu.VMEM((2, rows, FF_TILE), F32)],
        compiler_params=_params("parallel"),
        name="conv_ffn",
    )(h2, x1, mod, w_up, w_conv, b_conv, w_down)


def _layer(x, mod, mod_row, wts, ctx, gla_tile, ffn_seqs, fuse_merge_ffn):
    batch, seq, _ = x.shape
    x2 = x.reshape(batch * seq, D_MODEL)
    q, k, v, gr, la, nq, nk, nv, sg = _inproj(x2, mod, mod_row, wts["g_mix"], wts["w_t"], wts["w_lr"],
                                              wts["b_lr"], wts["g_qn"], wts["g_kn"], wts["seg"], batch, seq,
                                              transpose_kv=ctx is None)
    s0 = None if ctx is None else (ctx[2], ctx[3])
    of, ob, sf, sb = _gla(q, k, v, la, s0, batch, seq, *gla_tile)
    if ctx is None:
        o_na = _ctx_attn(nq, nk, nv, batch, seq)
    else:
        o_na = _na_attn(nq, nk, nv, ctx[0], ctx[1], ctx[4], batch, seq)
    merge_w = (wts["g_gla"], wts["w_gla_o"], wts["w_na_o"], wts["w_out"], wts["g_ffn"])
    ffn_w = (wts["w_up"], wts["w_conv"], wts["b_conv"], wts["w_down"])
    if fuse_merge_ffn:
        y = _merge_ffn(x2, mod, mod_row, of, ob, gr, o_na, sg, *merge_w, *ffn_w, batch, seq, ffn_seqs)
    else:
        x1, h2 = _merge(x2, mod, mod_row, of, ob, gr, o_na, sg, *merge_w)
        y = _ffn(h2, x1, mod, mod_row, *ffn_w, batch, seq, ffn_seqs)
    return y.reshape(batch, seq, D_MODEL), nk, nv, sf, sb


def _pack_weights(l, w_in, w_alpha_fwd, b_alpha_fwd, w_alpha_bwd, b_alpha_bwd, g_norm_mix, g_gla_norm,
                  g_q_norm, g_k_norm, w_gla_o, w_na_o, w_out, g_norm_ffn, w_up, w_conv, b_conv, w_down):
    w_t = _cast_bf16(jnp.swapaxes(w_in[l], 0, 1), CAST_ROWS)
    w_lr = jnp.zeros((LANES, 2 * GLA_QK), F32)
    w_lr = w_lr.at[:GLA_LOWRANK, :GLA_QK].set(w_alpha_fwd[l])
    w_lr = w_lr.at[GLA_LOWRANK:2 * GLA_LOWRANK, GLA_QK:].set(w_alpha_bwd[l]).astype(BF16)
    seg = jnp.asarray(np.kron(np.eye(NA_HEADS), np.ones((NA_HD, NA_HD))), BF16)
    return dict(
        g_mix=g_norm_mix[l][None], w_t=w_t, w_lr=w_lr,
        b_lr=jnp.concatenate([b_alpha_fwd[l], b_alpha_bwd[l]])[None],
        g_qn=jnp.tile(g_q_norm[l], NA_HEADS)[None], g_kn=jnp.tile(g_k_norm[l], NA_HEADS)[None], seg=seg,
        g_gla=jnp.tile(g_gla_norm[l], GLA_HEADS)[None],
        w_gla_o=w_gla_o[l].astype(BF16), w_na_o=w_na_o[l].astype(BF16), w_out=w_out[l].astype(BF16),
        g_ffn=g_norm_ffn[l][None], w_up=w_up[l].astype(BF16), w_conv=w_conv[l], b_conv=b_conv[l][None],
        w_down=w_down[l].astype(BF16))


def kernel(x_prompt, x_sample, c, cache_na_k, cache_na_v, state_gla_fwd, state_gla_bwd, c_ctx, w_ada, b_ada,
           g_norm_mix, w_in, w_alpha_fwd, b_alpha_fwd, w_alpha_bwd, b_alpha_bwd, g_gla_norm, g_q_norm, g_k_norm,
           rpb, w_gla_o, w_na_o, w_out, g_norm_ffn, w_up, w_conv, b_conv, w_down):
    depth = w_in.shape[0]
    batch, seq, _ = x_prompt.shape
    dec_batch, dec_seq, _ = x_sample.shape
    past = cache_na_k.shape[2]
    cond = jnp.concatenate([c_ctx[None], c], axis=0)
    xp, xs = x_prompt, x_sample
    new_k, new_v, new_sf, new_sb = [], [], [], []
    for l in range(depth):
        wts = _pack_weights(l, w_in, w_alpha_fwd, b_alpha_fwd, w_alpha_bwd, b_alpha_bwd, g_norm_mix, g_gla_norm,
                            g_q_norm, g_k_norm, w_gla_o, w_na_o, w_out, g_norm_ffn, w_up, w_conv, b_conv, w_down)
        mod = _ada(cond, w_ada[l], b_ada[l][None]).reshape(1 + dec_batch, 6, 1, D_MODEL)
        tbl = _bias_rows(rpb[l])
        xp, kc, vc, sf, sb = _layer(xp, mod, lambda tok: 0, wts, None, gla_tile=(4, 4), ffn_seqs=2,
                                    fuse_merge_ffn=True)
        to_cache = lambda a: a.reshape(batch, NA_HEADS, NA_HD, seq).transpose(0, 3, 1, 2)
        from_cache = lambda a: a.transpose(0, 2, 3, 1).reshape(dec_batch, NA_W, past)
        new_k.append(to_cache(kc))
        new_v.append(to_cache(vc))
        new_sf.append(sf)
        new_sb.append(sb)
        ctx = (from_cache(cache_na_k[:, l]), from_cache(cache_na_v[:, l]),
               state_gla_fwd[:, l], state_gla_bwd[:, l], tbl)
        xs, _, _, _, _ = _layer(xs, mod, lambda tok: 1 + tok // dec_seq, wts, ctx, gla_tile=(dec_batch, 4),
                               ffn_seqs=1, fuse_merge_ffn=False)
    return (xp, xs, jnp.stack(new_k, axis=1), jnp.stack(new_v, axis=1),
            jnp.stack(new_sf, axis=1), jnp.stack(new_sb, axis=1))
```

```python
import functools

import numpy as np
import jax
import jax.numpy as jnp
from jax import lax
from jax.experimental import pallas as pl
from jax.experimental.pallas import tpu as pltpu

F32 = jnp.float32
BF16 = jnp.bfloat16

D_MODEL = 1024
GRID_W = 64
GLA_HEADS, GLA_DK, GLA_DV = 4, 64, 128
GLA_QK = GLA_HEADS * GLA_DK
GLA_V = GLA_HEADS * GLA_DV
GLA_LOWRANK = 16
GLA_GATE_NORM = 16.0
GLA_CHUNK = 64
NA_HEADS, NA_HD = 8, 64
NA_W = NA_HEADS * NA_HD
NA_WIN_ROWS, NA_WIN_COLS = 8, 16
D_FF = 2816
EPS = 1e-6
IN_SIZES = (GLA_QK, GLA_QK, GLA_V, GLA_V, GLA_LOWRANK, GLA_LOWRANK, NA_W, NA_W, NA_W, D_MODEL, D_MODEL)

LANES = 128
ROW_TILE = 512
INPROJ_TILE = 512
FF_SLOT_ELEMS = 512 * 1024
MASK_VALUE = -1e30
VMEM_LIMIT = 56 * 1024 * 1024

NT_DIMS = (((1,), (1,)), ((), ()))
TN_DIMS = (((0,), (0,)), ((), ()))


def _dot(a, b):
    return jnp.dot(a, b, preferred_element_type=F32)


def _dot_nt(a, b):
    return lax.dot_general(a, b, NT_DIMS, preferred_element_type=F32)


def _dot_tn(a, b):
    return lax.dot_general(a, b, TN_DIMS, preferred_element_type=F32)


def _split3(a):
    t1 = a.astype(BF16)
    e1 = a - t1.astype(F32)
    t2 = e1.astype(BF16)
    t3 = (e1 - t2.astype(F32)).astype(BF16)
    return t1, t2, t3


def _params(*sem):
    return pltpu.CompilerParams(dimension_semantics=sem, vmem_limit_bytes=VMEM_LIMIT)


def _const_spec(shape):
    zeros = (0,) * len(shape)
    return pl.BlockSpec(shape, lambda *_: zeros, pipeline_mode=pl.Buffered(1))


def _ada_kernel(c_ref, w_ref, b_ref, o_ref):
    c = c_ref[...]
    s = c * jax.nn.sigmoid(c)
    o_ref[...] = _dot(s.astype(BF16), w_ref[...].astype(BF16)) + b_ref[...]


def _ada(cond, w_ada, b_ada):
    n = cond.shape[0]
    cols = 2 * D_MODEL
    return pl.pallas_call(
        _ada_kernel,
        grid=(6 * D_MODEL // cols,),
        in_specs=[pl.BlockSpec((n, D_MODEL), lambda j: (0, 0)),
                  pl.BlockSpec((D_MODEL, cols), lambda j: (0, j)),
                  pl.BlockSpec((1, cols), lambda j: (0, j))],
        out_specs=pl.BlockSpec((n, cols), lambda j: (0, j)),
        out_shape=jax.ShapeDtypeStruct((n, 6 * D_MODEL), F32),
        compiler_params=_params("arbitrary"),
        name="ada",
    )(cond, w_ada, b_ada)


def _rms(x):
    return x * lax.rsqrt(jnp.mean(x * x, axis=-1, keepdims=True) + EPS)


_W_OFF = [int(o) for o in np.cumsum((0,) + IN_SIZES)]
W_GLA, W_LR, W_NA, W_MERGE, W_END = _W_OFF[0], _W_OFF[4], _W_OFF[6], _W_OFF[9], _W_OFF[11]
CAST_ROWS = 736


def _cast_kernel(w_ref, o_ref):
    o_ref[...] = w_ref[...].astype(BF16)


def _cast_bf16(w, rows):
    n, c = w.shape
    assert n % rows == 0
    return pl.pallas_call(
        _cast_kernel,
        grid=(n // rows,),
        in_specs=[pl.BlockSpec((rows, c), lambda i: (i, 0))],
        out_specs=pl.BlockSpec((rows, c), lambda i: (i, 0)),
        out_shape=jax.ShapeDtypeStruct((n, c), BF16),
        compiler_params=_params("parallel"),
        name="cast_bf16",
    )(w)


def _inproj_kernel(transpose_kv, x_ref, mod_ref, g_ref, w_ref, wlr_ref, blr_ref, gqn_ref, gkn_ref, seg_ref,
                   q_ref, k_ref, v_ref, gr_ref, la_ref, nq_ref, nk_ref, nv_ref, sg_ref):
    h = _rms(x_ref[...]) * g_ref[...]
    h = (h * (1.0 + mod_ref[1]) + mod_ref[0]).astype(BF16)

    u = _dot_nt(h, w_ref[W_GLA:W_LR, :])
    q_ref[...] = (u[:, :GLA_QK] * (GLA_DK ** -0.5)).astype(BF16)
    k_ref[...] = u[:, GLA_QK:2 * GLA_QK].astype(BF16)
    v_ref[...] = u[:, 2 * GLA_QK:2 * GLA_QK + GLA_V].astype(BF16)
    gr = u[:, 2 * GLA_QK + GLA_V:]
    gr_ref[...] = (gr * jax.nn.sigmoid(gr)).astype(BF16)

    z = _dot(_dot_nt(h, w_ref[W_LR:W_LR + LANES, :]).astype(BF16), wlr_ref[...]) + blr_ref[...]
    log_sig = jnp.minimum(z, 0.0) - jnp.log1p(jnp.exp(-jnp.abs(z)))
    la_ref[...] = log_sig * (1.0 / GLA_GATE_NORM)

    u = _dot_nt(h, w_ref[W_NA:W_MERGE, :])

    def head_norm(a, gain):
        ssq = _dot((a * a).astype(BF16), seg_ref[...])
        return a * lax.rsqrt(ssq * (1.0 / NA_HD) + EPS) * gain

    nq_ref[...] = (head_norm(u[:, :NA_W], gqn_ref[...]) * (NA_HD ** -0.5)).astype(BF16)
    nk = head_norm(u[:, NA_W:2 * NA_W], gkn_ref[...])
    nv = u[:, 2 * NA_W:]
    if transpose_kv:
        seq = nk_ref.shape[2]
        for s in range(nk_ref.shape[0]):
            nk_ref[s] = nk[s * seq:(s + 1) * seq].T
            nv_ref[s] = nv[s * seq:(s + 1) * seq].T
    else:
        nk_ref[...] = nk.astype(nk_ref.dtype)
        nv_ref[...] = nv.astype(nv_ref.dtype)

    sg_ref[...] = jax.nn.sigmoid(_dot_nt(h, w_ref[W_MERGE:W_END, :])).astype(BF16)


def _inproj(x, mod, mod_row, g_mix, w_t, w_lr, b_lr, g_qn, g_kn, seg, batch, seq, transpose_kv):
    n = x.shape[0]
    tm = INPROJ_TILE
    assert (seq % tm == 0 or tm % seq == 0) and W_LR % 16 == 0 and W_NA % 16 == 0 and W_MERGE % 16 == 0
    row = lambda c: pl.BlockSpec((tm, c), lambda i: (i, 0))
    outs = [(GLA_QK, BF16), (GLA_QK, BF16), (GLA_V, BF16), (GLA_V, BF16), (2 * GLA_QK, F32), (NA_W, BF16)]
    out_specs = [row(c) for c, _ in outs]
    out_shape = [jax.ShapeDtypeStruct((n, c), dt) for c, dt in outs]
    if transpose_kv:
        if tm >= seq:
            kv_spec = pl.BlockSpec((tm // seq, NA_W, seq), lambda i: (i, 0, 0))
        else:
            tps = seq // tm
            kv_spec = pl.BlockSpec((1, NA_W, tm), lambda i: (i // tps, 0, i % tps))
        kv_shape = jax.ShapeDtypeStruct((batch, NA_W, seq), F32)
    else:
        kv_spec = row(NA_W)
        kv_shape = jax.ShapeDtypeStruct((n, NA_W), BF16)
    out_specs += [kv_spec, kv_spec, row(2 * D_MODEL)]
    out_shape += [kv_shape, kv_shape, jax.ShapeDtypeStruct((n, 2 * D_MODEL), BF16)]
    return pl.pallas_call(
        functools.partial(_inproj_kernel, transpose_kv),
        grid=(n // tm,),
        in_specs=[row(D_MODEL),
                  pl.BlockSpec((None, 6, 1, D_MODEL), lambda i: (mod_row(i * tm), 0, 0, 0)),
                  _const_spec(g_mix.shape), _const_spec(w_t.shape), _const_spec(w_lr.shape),
                  _const_spec(b_lr.shape), _const_spec(g_qn.shape), _const_spec(g_kn.shape),
                  _const_spec(seg.shape)],
        out_specs=out_specs,
        out_shape=out_shape,
        compiler_params=_params("parallel"),
        name="inproj",
    )(x, mod, g_mix, w_t, w_lr, b_lr, g_qn, g_kn, seg)


GLA_SAFE_DECAY = 80.0


def _chunk_cumsum(x, reverse):
    c = x.shape[0]
    rows = lax.broadcasted_iota(jnp.int32, (c, 1), 0)
    s = 1
    while s < c:
        if reverse:
            x = x + jnp.where(rows < c - s, pltpu.roll(x, c - s, axis=0), 0.0)
        else:
            x = x + jnp.where(rows >= s, pltpu.roll(x, s, axis=0), 0.0)
        s *= 2
    return x


def _gla_kernel(has_s0, nsteps, cps, *refs):
    qf, kf, vf, laf, qb, kb, vb, lab = refs[:8]
    if has_s0:
        s0f, s0b, of, ob, sf, sb, st, oi, q32, bcs = refs[8:]
    else:
        of, ob, sf, sb, st, oi, q32, bcs = refs[8:]
        s0f = s0b = None
    n = pl.program_id(1)
    bt = qf.shape[0]
    C, H, DK, DV = GLA_CHUNK, GLA_HEADS, GLA_DK, GLA_DV
    assert C == DK
    HC = H * C
    row = lax.broadcasted_iota(jnp.int32, (HC, HC), 0)
    col = lax.broadcasted_iota(jnp.int32, (HC, HC), 1)
    same_head = (row // C) == (col // C)
    t_in, s_in = row % C, col % C
    vis = (same_head & (t_in >= s_in), same_head & (t_in <= s_in))
    ins = ((qf, kf, vf, laf, of), (qb, kb, vb, lab, ob))
    order = (list(range(cps)), list(range(cps))[::-1])
    units = [(d, b, c) for c in range(cps) for d in range(2) for b in range(bt)]
    heads = [(slice(h * DK, (h + 1) * DK), slice(h * DV, (h + 1) * DV)) for h in range(H)]

    def chunk(d, which, b, c):
        return ins[d][which][b, c * C:(c + 1) * C, :]

    def rows4(a):
        return jnp.concatenate([a] * H, axis=0)

    def block_diag(a):
        return jnp.where(same_head, rows4(a), jnp.zeros((), a.dtype))

    def v_stack(v):
        return jnp.concatenate([v[:, vs] for _, vs in heads], axis=0)

    @pl.when(n == 0)
    def _init():
        for d, s0 in enumerate((s0f, s0b)):
            for b in range(bt):
                for h in range(H):
                    st[d, b, h * DK:(h + 1) * DK, :] = s0[b, h] if has_s0 else jnp.zeros((DK, DV), F32)

    bc, tot, q_bd, kh_bd, et = {}, {}, {}, {}, {}
    for u in units:
        d, b, c = u
        bc[u] = _chunk_cumsum(chunk(d, 3, b, c), reverse=d == 1)
        tot[u] = bc[u][C - 1:C, :] if d == 0 else bc[u][0:1, :]
    worst = functools.reduce(jnp.maximum, [jnp.max(-tot[u]) for u in units])
    safe = worst <= GLA_SAFE_DECAY

    for u in units:
        d, b, c = u
        q = chunk(d, 0, b, c).astype(F32)
        k = chunk(d, 1, b, c).astype(F32)
        q_bd[u] = block_diag((q * jnp.exp(bc[u])).astype(BF16))
        kh_bd[u] = block_diag((k * jnp.exp(tot[u] - bc[u])).astype(BF16))
        et[u] = jnp.exp(tot[u])

    @pl.when(safe)
    def _intra_factorised():
        for u in units:
            d, b, c = u
            k = chunk(d, 1, b, c).astype(F32)
            kt = rows4((k * jnp.exp(-bc[u])).astype(BF16))
            s = jnp.where(vis[d], _dot_nt(q_bd[u], kt), 0.0).astype(BF16)
            oi[d, b, c] = _dot(s, v_stack(chunk(d, 2, b, c)))

    @pl.when(jnp.logical_not(safe))
    def _intra_pairwise():
        for u in units:
            d, b, c = u
            q32[d, b, c] = chunk(d, 0, b, c).astype(F32)
            bcs[d, b, c] = bc[u]
        svis = lax.broadcasted_iota(jnp.int32, (C, 1), 0)
        for u in units:
            d, b, c = u
            k = chunk(d, 1, b, c).astype(F32)
            v = chunk(d, 2, b, c).astype(F32)
            for h, (ks, vs) in enumerate(heads):
                kh_, bh, vh = k[:, ks], bc[u][:, ks], v[:, vs]

                def one_row(t, carry, d=d, b=b, c=c, h=h, ks=ks, kh_=kh_, bh=bh, vh=vh):
                    bt_row = bcs[d, b, c, pl.ds(t, 1), :][:, ks]
                    q_row = q32[d, b, c, pl.ds(t, 1), :][:, ks]
                    seen = (svis <= t) if d == 0 else (svis >= t)
                    w = jnp.exp(jnp.where(seen, bt_row - bh, -jnp.inf)) * kh_ * q_row
                    s_col = jnp.sum(w, axis=1, keepdims=True)
                    oi[d, b, c, pl.ds(h * C + t, 1), :] = jnp.sum(s_col * vh, axis=0, keepdims=True)
                    return carry

                lax.fori_loop(0, C, one_row, 0)

    ones = jnp.ones((16, DV), BF16)
    for j in range(cps):
        for d in range(2):
            c = order[d][j]
            for b in range(bt):
                u = (d, b, c)
                o = oi[d, b, c] + _dot(q_bd[u], st[d, b].astype(BF16))
                ins[d][4][b, c * C:(c + 1) * C, :] = jnp.concatenate(
                    [o[h * C:(h + 1) * C] for h in range(H)], axis=1).astype(ins[d][4].dtype)
        for d in range(2):
            c = order[d][j]
            for b in range(bt):
                u = (d, b, c)
                e_rows = jnp.concatenate(list(_split3(et[u])) + [jnp.zeros((13, H * DK), BF16)], axis=0)
                decay = _dot_tn(e_rows, ones)
                st[d, b] = decay * st[d, b] + _dot_tn(kh_bd[u], v_stack(chunk(d, 2, b, c)))

    @pl.when(n == nsteps - 1)
    def _fin():
        for b in range(bt):
            for h in range(H):
                sf[b, h] = st[0, b, h * DK:(h + 1) * DK, :]
                sb[b, h] = st[1, b, h * DK:(h + 1) * DK, :]


def _gla(q, k, v, la, s0, batch, seq, bt, cps):
    C = GLA_CHUNK
    rows = cps * C
    assert seq % rows == 0 and batch % bt == 0
    ns = seq // rows
    q, k, v, la = (a.reshape(batch, seq, a.shape[-1]) for a in (q, k, v, la))
    fwd = lambda c, j=0: pl.BlockSpec((bt, rows, c), lambda i, n: (i, n, j))
    bwd = lambda c, j=0: pl.BlockSpec((bt, rows, c), lambda i, n: (i, ns - 1 - n, j))
    st_spec = pl.BlockSpec((bt, GLA_HEADS, GLA_DK, GLA_DV), lambda i, n: (i, 0, 0, 0))
    in_specs = [fwd(GLA_QK), fwd(GLA_QK), fwd(GLA_V), fwd(GLA_QK, 0),
                bwd(GLA_QK), bwd(GLA_QK), bwd(GLA_V), bwd(GLA_QK, 1)]
    args = [q, k, v, la, q, k, v, la]
    if s0 is not None:
        in_specs += [st_spec, st_spec]
        args += list(s0)
    st_shape = jax.ShapeDtypeStruct((batch, GLA_HEADS, GLA_DK, GLA_DV), F32)
    o_shape = jax.ShapeDtypeStruct((batch, seq, GLA_V), BF16)
    of, ob, sf, sb = pl.pallas_call(
        functools.partial(_gla_kernel, s0 is not None, ns, cps),
        grid=(batch // bt, ns),
        in_specs=in_specs,
        out_specs=[fwd(GLA_V), bwd(GLA_V), st_spec, st_spec],
        out_shape=[o_shape, o_shape, st_shape, st_shape],
        scratch_shapes=[pltpu.VMEM((2, bt, GLA_QK, GLA_DV), F32),
                        pltpu.VMEM((2, bt, cps, GLA_HEADS * C, GLA_DV), F32),
                        pltpu.VMEM((2, bt, cps, C, GLA_QK), F32),
                        pltpu.VMEM((2, bt, cps, C, GLA_QK), F32)],
        compiler_params=_params("parallel", "arbitrary"),
        name="gla",
    )(*args)
    return of.reshape(batch * seq, GLA_V), ob.reshape(batch * seq, GLA_V), sf, sb


def _softmax_pv(scores, values):
    m = functools.reduce(jnp.maximum, [jnp.max(s, axis=-1, keepdims=True) for s in scores])
    ps = [jnp.exp(s - m) for s in scores]
    l = functools.reduce(jnp.add, [jnp.sum(p, axis=-1, keepdims=True) for p in ps])
    o = functools.reduce(jnp.add, [(_dot_nt if transposed else _dot)(p.astype(BF16), v)
                                   for p, (v, transposed) in zip(ps, values)])
    return o / l


CTX_ATTN_SEQS = 2


def _ctx_attn_kernel(q_ref, kt_ref, vt_ref, o_ref):
    seq = kt_ref.shape[2]
    for s in range(kt_ref.shape[0]):
        q = q_ref[s * seq:(s + 1) * seq, :]
        kt = kt_ref[s].astype(BF16)
        vt = vt_ref[s].astype(BF16)
        outs = []
        for h in range(NA_HEADS):
            hs = slice(h * NA_HD, (h + 1) * NA_HD)
            outs.append(_softmax_pv([_dot(q[:, hs], kt[hs, :])], [(vt[hs, :], True)]))
        o_ref[s * seq:(s + 1) * seq, :] = jnp.concatenate(outs, axis=-1).astype(o_ref.dtype)


def _ctx_attn(q, kt, vt, batch, seq):
    n = CTX_ATTN_SEQS
    assert batch % n == 0
    spec = pl.BlockSpec((n * seq, NA_W), lambda b: (b, 0))
    spec_t = pl.BlockSpec((n, NA_W, seq), lambda b: (b, 0, 0))
    return pl.pallas_call(
        _ctx_attn_kernel,
        grid=(batch // n,),
        in_specs=[spec, spec_t, spec_t],
        out_specs=spec,
        out_shape=jax.ShapeDtypeStruct((batch * seq, NA_W), BF16),
        compiler_params=_params("parallel"),
        name="ctx_attn",
    )(q, kt, vt)


N_DR = 2 * NA_WIN_ROWS - 1
N_DC = 2 * NA_WIN_COLS - 1
NA_QROWS = 4
NA_KROWS = NA_WIN_ROWS + NA_QROWS
BIAS_BOTH = 0
BIAS_RIGHT = N_DR - 1
BIAS_LEFT = BIAS_RIGHT + NA_WIN_ROWS
BIAS_NONE = BIAS_LEFT + NA_WIN_ROWS
BIAS_ROWS = 32


def _bias_rows(rpb):
    assert 2 * GRID_W == LANES and BIAS_NONE < BIAS_ROWS
    b = jnp.pad(rpb, ((0, 0), (0, 0), (0, GRID_W - N_DC)))
    m = jnp.full((NA_HEADS, NA_WIN_ROWS, GRID_W), MASK_VALUE, F32)
    both = jnp.concatenate([b[:, :-1], b[:, 1:]], axis=-1)
    right = jnp.concatenate([m, b[:, :NA_WIN_ROWS]], axis=-1)
    left = jnp.concatenate([b[:, NA_WIN_ROWS - 1:], m], axis=-1)
    none = jnp.concatenate([m[:, :1], m[:, :1]], axis=-1)
    pad = jnp.zeros((NA_HEADS, BIAS_ROWS - BIAS_NONE - 1, LANES), F32)
    return jnp.concatenate([both, right, left, none, pad], axis=1).reshape(NA_HEADS * BIAS_ROWS, LANES)


def _na_kernel(rows, q_ref, k_ref, v_ref, kct_ref, vct_ref, bias_ref, o_ref, ctx_ref):
    g = pl.program_id(1)

    @pl.when(g == 0)
    def _cast_ctx():
        ctx_ref[0] = kct_ref[...].astype(BF16)
        ctx_ref[1] = vct_ref[...].astype(BF16)

    wr = NA_WIN_ROWS
    ws = jnp.clip(g * NA_QROWS - wr // 2, 0, rows - NA_KROWS)
    start = pl.multiple_of(ws * GRID_W, GRID_W)
    q = q_ref[...]
    kl = k_ref[pl.ds(start, NA_KROWS * GRID_W), :]
    vl = v_ref[pl.ds(start, NA_KROWS * GRID_W), :]

    tile_row = []
    for i in range(NA_QROWS):
        r = g * NA_QROWS + i
        r0 = jnp.clip(r - wr // 2, 0, rows - wr)
        for j in range(NA_KROWS // 2):
            kr = ws + 2 * j
            in0 = (kr >= r0) & (kr < r0 + wr)
            in1 = (kr + 1 >= r0) & (kr + 1 < r0 + wr)
            dr = kr - r + wr - 1
            tile_row.append(jnp.where(in0 & in1, BIAS_BOTH + dr,
                                      jnp.where(in1, BIAS_RIGHT + dr + 1,
                                                jnp.where(in0, BIAS_LEFT + dr - (wr - 1), BIAS_NONE))))

    qc = lax.broadcasted_iota(jnp.int32, (GRID_W, LANES), 0)
    kc = lax.broadcasted_iota(jnp.int32, (GRID_W, LANES), 1) % GRID_W
    c0 = jnp.clip(qc - NA_WIN_COLS // 2, 0, GRID_W - NA_WIN_COLS)
    in_win = (kc >= c0) & (kc < c0 + NA_WIN_COLS)
    shift = LANES - (NA_WIN_COLS - 1)

    def bias_tile(h, t):
        row = jnp.broadcast_to(bias_ref[pl.ds(h * BIAS_ROWS + t, 1), :], (GRID_W, LANES))
        return jnp.where(in_win, pltpu.roll(row, shift, axis=1, stride=1, stride_axis=0), MASK_VALUE)

    outs = []
    for h in range(NA_HEADS):
        hs = slice(h * NA_HD, (h + 1) * NA_HD)
        bias = jnp.concatenate(
            [jnp.concatenate([bias_tile(h, tile_row[i * (NA_KROWS // 2) + j]) for j in range(NA_KROWS // 2)], axis=1)
             for i in range(NA_QROWS)], axis=0)
        s_loc = _dot_nt(q[:, hs], kl[:, hs]) + bias
        s_ctx = _dot(q[:, hs], ctx_ref[0, hs, :])
        outs.append(_softmax_pv([s_loc, s_ctx], [(vl[:, hs], False), (ctx_ref[1, hs, :], True)]))
    o_ref[...] = jnp.concatenate(outs, axis=-1).astype(o_ref.dtype)


def _na_attn(q, k, v, kt_ctx, vt_ctx, bias_rows, batch, seq):
    rows = seq // GRID_W
    wr = NA_WIN_ROWS
    assert rows >= NA_KROWS and rows % NA_QROWS == 0 and wr % 2 == 0 and NA_KROWS % 2 == 0
    for g in range(rows // NA_QROWS):
        ws = min(max(g * NA_QROWS - wr // 2, 0), rows - NA_KROWS)
        for r in range(g * NA_QROWS, (g + 1) * NA_QROWS):
            r0 = min(max(r - wr // 2, 0), rows - wr)
            assert ws <= r0 and r0 + wr <= ws + NA_KROWS
    past = kt_ctx.shape[2]
    q, k, v = (a.reshape(batch, seq, NA_W) for a in (q, k, v))
    whole = pl.BlockSpec((None, seq, NA_W), lambda b, g: (b, 0, 0))
    whole_t = pl.BlockSpec((None, NA_W, past), lambda b, g: (b, 0, 0))
    tile = pl.BlockSpec((None, NA_QROWS * GRID_W, NA_W), lambda b, g: (b, g, 0))
    o = pl.pallas_call(
        functools.partial(_na_kernel, rows),
        grid=(batch, rows // NA_QROWS),
        in_specs=[tile, whole, whole, whole_t, whole_t, _const_spec(bias_rows.shape)],
        out_specs=tile,
        out_shape=jax.ShapeDtypeStruct((batch, seq, NA_W), BF16),
        scratch_shapes=[pltpu.VMEM((2, NA_W, past), BF16)],
        compiler_params=_params("parallel", "arbitrary"),
        name="na_attn",
    )(q, k, v, kt_ctx, vt_ctx, bias_rows)
    return o.reshape(batch * seq, NA_W)


def _merge_body(x, mod_ref, of_ref, ob_ref, gr_ref, na_ref, sg_ref, ggla_ref, wg_ref, wn_ref, wo_ref, gffn_ref):
    o = of_ref[...].astype(F32) + ob_ref[...].astype(F32)
    on = jnp.concatenate([_rms(o[:, h * GLA_DV:(h + 1) * GLA_DV]) for h in range(GLA_HEADS)], axis=-1)
    on = on * ggla_ref[...] * gr_ref[...].astype(F32)
    b_gla = _dot(on.astype(BF16), wg_ref[...])
    b_na = _dot(na_ref[...], wn_ref[...])
    sg = sg_ref[...].astype(F32)
    mix = _dot((sg[:, :D_MODEL] * b_gla + sg[:, D_MODEL:] * b_na).astype(BF16), wo_ref[...])
    x1 = x + mod_ref[2] * mix
    h2 = (_rms(x1) * gffn_ref[...] * (1.0 + mod_ref[4]) + mod_ref[3]).astype(BF16)
    return x1, h2


def _merge_kernel(x_ref, mod_ref, of_ref, ob_ref, gr_ref, na_ref, sg_ref, ggla_ref, wg_ref, wn_ref, wo_ref,
                  gffn_ref, x1_ref, h2_ref):
    x1_ref[...], h2_ref[...] = _merge_body(x_ref[...], mod_ref, of_ref, ob_ref, gr_ref, na_ref, sg_ref, ggla_ref,
                                           wg_ref, wn_ref, wo_ref, gffn_ref)


def _merge(x, mod, mod_row, of, ob, gr, na, sg, g_gla, w_gla_o, w_na_o, w_out, g_ffn):
    n = x.shape[0]
    tm = ROW_TILE
    row = lambda c: pl.BlockSpec((tm, c), lambda i: (i, 0))
    return pl.pallas_call(
        _merge_kernel,
        grid=(n // tm,),
        in_specs=[row(D_MODEL),
                  pl.BlockSpec((None, 6, 1, D_MODEL), lambda i: (mod_row(i * tm), 0, 0, 0)),
                  row(GLA_V), row(GLA_V), row(GLA_V), row(NA_W), row(2 * D_MODEL),
                  _const_spec(g_gla.shape), _const_spec(w_gla_o.shape), _const_spec(w_na_o.shape),
                  _const_spec(w_out.shape), _const_spec(g_ffn.shape)],
        out_specs=[row(D_MODEL), row(D_MODEL)],
        out_shape=[jax.ShapeDtypeStruct((n, D_MODEL), F32), jax.ShapeDtypeStruct((n, D_MODEL), BF16)],
        compiler_params=_params("parallel"),
        name="merge",
    )(x, mod, of, ob, gr, na, sg, g_gla, w_gla_o, w_na_o, w_out, g_ffn)


def _ffn_body(seq, h, wup_ref, wc_ref, bc_ref, wdn_ref, acc_ref, a_ref, g_ref):
    t = h.shape[0]
    pos = lax.broadcasted_iota(jnp.int32, (t, 1), 0) % seq
    first = pos == 0
    last = pos == seq - 1
    ff_tile = a_ref.shape[2]
    tiles = [(lo, min(lo + ff_tile, D_FF)) for lo in range(0, D_FF, ff_tile)]

    def up(j):
        lo, hi = tiles[j]
        a_ref[j % 2, :, :hi - lo] = _dot(h, wup_ref[:, lo:hi])
        g_ref[j % 2, :, :hi - lo] = _dot(h, wup_ref[:, D_FF + lo:D_FF + hi])

    def conv(u, lo, hi):
        prev = jnp.where(first, 0.0, pltpu.roll(u, 1, axis=0))
        nxt = jnp.where(last, 0.0, pltpu.roll(u, t - 1, axis=0))
        return wc_ref[0:1, lo:hi] * prev + wc_ref[1:2, lo:hi] * u + wc_ref[2:3, lo:hi] * nxt + bc_ref[:, lo:hi]

    up(0)
    for j, (lo, hi) in enumerate(tiles):
        if j + 1 < len(tiles):
            up(j + 1)
        a = conv(a_ref[j % 2, :, :hi - lo], lo, hi)
        g = conv(g_ref[j % 2, :, :hi - lo], D_FF + lo, D_FF + hi)
        part = _dot((a * (g * jax.nn.sigmoid(g))).astype(BF16), wdn_ref[lo:hi, :])
        if j == 0:
            acc_ref[...] = part
        else:
            acc_ref[...] += part


def _ffn_kernel(seq, h_ref, x1_ref, mod_ref, wup_ref, wc_ref, bc_ref, wdn_ref, o_ref, a_ref, g_ref):
    _ffn_body(seq, h_ref[...], wup_ref, wc_ref, bc_ref, wdn_ref, o_ref, a_ref, g_ref)
    o_ref[...] = x1_ref[...] + mod_ref[5] * o_ref[...]


def _merge_ffn_kernel(seq, x_ref, mod_ref, of_ref, ob_ref, gr_ref, na_ref, sg_ref, ggla_ref, wg_ref, wn_ref, wo_ref,
                      gffn_ref, wup_ref, wc_ref, bc_ref, wdn_ref, o_ref, x1_ref, a_ref, g_ref):
    x1_ref[...], h2 = _merge_body(x_ref[...], mod_ref, of_ref, ob_ref, gr_ref, na_ref, sg_ref, ggla_ref,
                                  wg_ref, wn_ref, wo_ref, gffn_ref)
    _ffn_body(seq, h2, wup_ref, wc_ref, bc_ref, wdn_ref, o_ref, a_ref, g_ref)
    o_ref[...] = x1_ref[...] + mod_ref[5] * o_ref[...]


def _merge_ffn(x, mod, mod_row, of, ob, gr, na, sg, g_gla, w_gla_o, w_na_o, w_out, g_ffn, w_up, w_conv, b_conv,
               w_down, batch, seq, seqs_per_step):
    rows = seq * seqs_per_step
    row = lambda c: pl.BlockSpec((rows, c), lambda i: (i, 0))
    consts = (g_gla, w_gla_o, w_na_o, w_out, g_ffn, w_up, w_conv, b_conv, w_down)
    return pl.pallas_call(
        functools.partial(_merge_ffn_kernel, seq),
        grid=(batch // seqs_per_step,),
        in_specs=[row(D_MODEL),
                  pl.BlockSpec((None, 6, 1, D_MODEL), lambda i: (mod_row(i * rows), 0, 0, 0)),
                  row(GLA_V), row(GLA_V), row(GLA_V), row(NA_W), row(2 * D_MODEL)]
                 + [_const_spec(c.shape) for c in consts],
        out_specs=row(D_MODEL),
        out_shape=jax.ShapeDtypeStruct((batch * seq, D_MODEL), F32),
        scratch_shapes=[pltpu.VMEM((rows, D_MODEL), F32),
                        pltpu.VMEM((2, rows, FF_SLOT_ELEMS // rows), F32),
                        pltpu.VMEM((2, rows, FF_SLOT_ELEMS // rows), F32)],
        compiler_params=_params("parallel"),
        name="merge_ffn",
    )(x, mod, of, ob, gr, na, sg, *consts)


def _ffn(h2, x1, mod, mod_row, w_up, w_conv, b_conv, w_down, batch, seq, seqs_per_step):
    rows = seq * seqs_per_step
    blk = lambda: pl.BlockSpec((rows, D_MODEL), lambda b: (b, 0))
    return pl.pallas_call(
        functools.partial(_ffn_kernel, seq),
        grid=(batch // seqs_per_step,),
        in_specs=[blk(), blk(),
                  pl.BlockSpec((None, 6, 1, D_MODEL), lambda b: (mod_row(b * rows), 0, 0, 0)),
                  _const_spec(w_up.shape), _const_spec(w_conv.shape), _const_spec(b_conv.shape),
                  _const_spec(w_down.shape)],
        out_specs=blk(),
        out_shape=jax.ShapeDtypeStruct((batch * seq, D_MODEL), F32),
        scratch_shapes=[pltpu.VMEM((2, rows, FF_SLOT_ELEMS // rows), F32),
                        pltpu.VMEM((2, rows, FF_SLOT_ELEMS // rows), F32)],
        compiler_params=_params("parallel"),
        name="conv_ffn",
    )(h2, x1, mod, w_up, w_conv, b_conv, w_down)


def _layer(x, mod, mod_row, wts, ctx, gla_tile, ffn_seqs, fuse_merge_ffn):
    batch, seq, _ = x.shape
    x2 = x.reshape(batch * seq, D_MODEL)
    q, k, v, gr, la, nq, nk, nv, sg = _inproj(x2, mod, mod_row, wts["g_mix"], wts["w_t"], wts["w_lr"],
                                              wts["b_lr"], wts["g_qn"], wts["g_kn"], wts["seg"], batch, seq,
                                              transpose_kv=ctx is None)
    s0 = None if ctx is None else (ctx[2], ctx[3])
    of, ob, sf, sb = _gla(q, k, v, la, s0, batch, seq, *gla_tile)
    if ctx is None:
        o_na = _ctx_attn(nq, nk, nv, batch, seq)
    else:
        o_na = _na_attn(nq, nk, nv, ctx[0], ctx[1], ctx[4], batch, seq)
    merge_w = (wts["g_gla"], wts["w_gla_o"], wts["w_na_o"], wts["w_out"], wts["g_ffn"])
    ffn_w = (wts["w_up"], wts["w_conv"], wts["b_conv"], wts["w_down"])
    if fuse_merge_ffn:
        y = _merge_ffn(x2, mod, mod_row, of, ob, gr, o_na, sg, *merge_w, *ffn_w, batch, seq, ffn_seqs)
    else:
        x1, h2 = _merge(x2, mod, mod_row, of, ob, gr, o_na, sg, *merge_w)
        y = _ffn(h2, x1, mod, mod_row, *ffn_w, batch, seq, ffn_seqs)
    return y.reshape(batch, seq, D_MODEL), nk, nv, sf, sb


def _pack_weights(l, w_in, w_alpha_fwd, b_alpha_fwd, w_alpha_bwd, b_alpha_bwd, g_norm_mix, g_gla_norm,
                  g_q_norm, g_k_norm, w_gla_o, w_na_o, w_out, g_norm_ffn, w_up, w_conv, b_conv, w_down):
    w_t = _cast_bf16(jnp.swapaxes(w_in[l], 0, 1), CAST_ROWS)
    w_lr = jnp.zeros((LANES, 2 * GLA_QK), F32)
    w_lr = w_lr.at[:GLA_LOWRANK, :GLA_QK].set(w_alpha_fwd[l])
    w_lr = w_lr.at[GLA_LOWRANK:2 * GLA_LOWRANK, GLA_QK:].set(w_alpha_bwd[l]).astype(BF16)
    seg = jnp.asarray(np.kron(np.eye(NA_HEADS), np.ones((NA_HD, NA_HD))), BF16)
    return dict(
        g_mix=g_norm_mix[l][None], w_t=w_t, w_lr=w_lr,
        b_lr=jnp.concatenate([b_alpha_fwd[l], b_alpha_bwd[l]])[None],
        g_qn=jnp.tile(g_q_norm[l], NA_HEADS)[None], g_kn=jnp.tile(g_k_norm[l], NA_HEADS)[None], seg=seg,
        g_gla=jnp.tile(g_gla_norm[l], GLA_HEADS)[None],
        w_gla_o=w_gla_o[l].astype(BF16), w_na_o=w_na_o[l].astype(BF16), w_out=w_out[l].astype(BF16),
        g_ffn=g_norm_ffn[l][None], w_up=w_up[l].astype(BF16), w_conv=w_conv[l], b_conv=b_conv[l][None],
        w_down=w_down[l].astype(BF16))


def kernel(x_prompt, x_sample, c, cache_na_k, cache_na_v, state_gla_fwd, state_gla_bwd, c_ctx, w_ada, b_ada,
           g_norm_mix, w_in, w_alpha_fwd, b_alpha_fwd, w_alpha_bwd, b_alpha_bwd, g_gla_norm, g_q_norm, g_k_norm,
           rpb, w_gla_o, w_na_o, w_out, g_norm_ffn, w_up, w_conv, b_conv, w_down):
    depth = w_in.shape[0]
    batch, seq, _ = x_prompt.shape
    dec_batch, dec_seq, _ = x_sample.shape
    past = cache_na_k.shape[2]
    cond = jnp.concatenate([c_ctx[None], c], axis=0)
    xp, xs = x_prompt, x_sample
    new_k, new_v, new_sf, new_sb = [], [], [], []
    for l in range(depth):
        wts = _pack_weights(l, w_in, w_alpha_fwd, b_alpha_fwd, w_alpha_bwd, b_alpha_bwd, g_norm_mix, g_gla_norm,
                            g_q_norm, g_k_norm, w_gla_o, w_na_o, w_out, g_norm_ffn, w_up, w_conv, b_conv, w_down)
        mod = _ada(cond, w_ada[l], b_ada[l][None]).reshape(1 + dec_batch, 6, 1, D_MODEL)
        tbl = _bias_rows(rpb[l])
        xp, kc, vc, sf, sb = _layer(xp, mod, lambda tok: 0, wts, None, gla_tile=(4, 4), ffn_seqs=2,
                                    fuse_merge_ffn=True)
        to_cache = lambda a: a.reshape(batch, NA_HEADS, NA_HD, seq).transpose(0, 3, 1, 2)
        from_cache = lambda a: a.transpose(0, 2, 3, 1).reshape(dec_batch, NA_W, past)
        new_k.append(to_cache(kc))
        new_v.append(to_cache(vc))
        new_sf.append(sf)
        new_sb.append(sb)
        ctx = (from_cache(cache_na_k[:, l]), from_cache(cache_na_v[:, l]),
               state_gla_fwd[:, l], state_gla_bwd[:, l], tbl)
        xs, _, _, _, _ = _layer(xs, mod, lambda tok: 1 + tok // dec_seq, wts, ctx, gla_tile=(dec_batch, 4),
                               ffn_seqs=1, fuse_merge_ffn=False)
    return (xp, xs, jnp.stack(new_k, axis=1), jnp.stack(new_v, axis=1),
            jnp.stack(new_sf, axis=1), jnp.stack(new_sb, axis=1))
```

```python
import functools

import numpy as np
import jax
import jax.numpy as jnp
from jax import lax
from jax.experimental import pallas as pl
from jax.experimental.pallas import tpu as pltpu

F32 = jnp.float32
BF16 = jnp.bfloat16

D_MODEL = 1024
GRID_W = 64
GLA_HEADS, GLA_DK, GLA_DV = 4, 64, 128
GLA_QK = GLA_HEADS * GLA_DK
GLA_V = GLA_HEADS * GLA_DV
GLA_LOWRANK = 16
GLA_GATE_NORM = 16.0
GLA_CHUNK = 64
NA_HEADS, NA_HD = 8, 64
NA_W = NA_HEADS * NA_HD
NA_WIN_ROWS, NA_WIN_COLS = 8, 16
D_FF = 2816
EPS = 1e-6
IN_SIZES = (GLA_QK, GLA_QK, GLA_V, GLA_V, GLA_LOWRANK, GLA_LOWRANK, NA_W, NA_W, NA_W, D_MODEL, D_MODEL)

LANES = 128
ROW_TILE = 512
INPROJ_TILE = 512
FF_TILE = 1024
MASK_VALUE = -1e30
VMEM_LIMIT = 56 * 1024 * 1024

NT_DIMS = (((1,), (1,)), ((), ()))
TN_DIMS = (((0,), (0,)), ((), ()))


def _dot(a, b):
    return jnp.dot(a, b, preferred_element_type=F32)


def _dot_nt(a, b):
    return lax.dot_general(a, b, NT_DIMS, preferred_element_type=F32)


def _dot_tn(a, b):
    return lax.dot_general(a, b, TN_DIMS, preferred_element_type=F32)


def _split3(a):
    t1 = a.astype(BF16)
    e1 = a - t1.astype(F32)
    t2 = e1.astype(BF16)
    t3 = (e1 - t2.astype(F32)).astype(BF16)
    return t1, t2, t3


def _params(*sem):
    return pltpu.CompilerParams(dimension_semantics=sem, vmem_limit_bytes=VMEM_LIMIT)


def _const_spec(shape):
    zeros = (0,) * len(shape)
    return pl.BlockSpec(shape, lambda *_: zeros, pipeline_mode=pl.Buffered(1))


def _ada_kernel(c_ref, w_ref, b_ref, o_ref):
    c = c_ref[...]
    s = c * jax.nn.sigmoid(c)
    o_ref[...] = _dot(s.astype(BF16), w_ref[...].astype(BF16)) + b_ref[...]


def _ada(cond, w_ada, b_ada):
    n = cond.shape[0]
    return pl.pallas_call(
        _ada_kernel,
        grid=(6,),
        in_specs=[pl.BlockSpec((n, D_MODEL), lambda j: (0, 0)),
                  pl.BlockSpec((D_MODEL, D_MODEL), lambda j: (0, j)),
                  pl.BlockSpec((1, D_MODEL), lambda j: (0, j))],
        out_specs=pl.BlockSpec((n, D_MODEL), lambda j: (0, j)),
        out_shape=jax.ShapeDtypeStruct((n, 6 * D_MODEL), F32),
        compiler_params=_params("arbitrary"),
        name="ada",
    )(cond, w_ada, b_ada)


def _rms(x):
    return x * lax.rsqrt(jnp.mean(x * x, axis=-1, keepdims=True) + EPS)


_W_OFF = [int(o) for o in np.cumsum((0,) + IN_SIZES)]
W_GLA, W_LR, W_NA, W_MERGE, W_END = _W_OFF[0], _W_OFF[4], _W_OFF[6], _W_OFF[9], _W_OFF[11]
CAST_ROWS = 736


def _cast_kernel(w_ref, o_ref):
    o_ref[...] = w_ref[...].astype(BF16)


def _cast_bf16(w, rows):
    n, c = w.shape
    assert n % rows == 0
    return pl.pallas_call(
        _cast_kernel,
        grid=(n // rows,),
        in_specs=[pl.BlockSpec((rows, c), lambda i: (i, 0))],
        out_specs=pl.BlockSpec((rows, c), lambda i: (i, 0)),
        out_shape=jax.ShapeDtypeStruct((n, c), BF16),
        compiler_params=_params("parallel"),
        name="cast_bf16",
    )(w)


def _inproj_kernel(transpose_kv, x_ref, mod_ref, g_ref, w_ref, wlr_ref, blr_ref, gqn_ref, gkn_ref, seg_ref,
                   q_ref, k_ref, v_ref, gr_ref, la_ref, nq_ref, nk_ref, nv_ref, sg_ref):
    h = _rms(x_ref[...]) * g_ref[...]
    h = (h * (1.0 + mod_ref[1]) + mod_ref[0]).astype(BF16)

    u = _dot_nt(h, w_ref[W_GLA:W_LR, :])
    q_ref[...] = (u[:, :GLA_QK] * (GLA_DK ** -0.5)).astype(BF16)
    k_ref[...] = u[:, GLA_QK:2 * GLA_QK].astype(BF16)
    v_ref[...] = u[:, 2 * GLA_QK:2 * GLA_QK + GLA_V].astype(BF16)
    gr = u[:, 2 * GLA_QK + GLA_V:]
    gr_ref[...] = (gr * jax.nn.sigmoid(gr)).astype(BF16)

    z = _dot(_dot_nt(h, w_ref[W_LR:W_LR + LANES, :]).astype(BF16), wlr_ref[...]) + blr_ref[...]
    log_sig = jnp.minimum(z, 0.0) - jnp.log1p(jnp.exp(-jnp.abs(z)))
    la_ref[...] = log_sig * (1.0 / GLA_GATE_NORM)

    u = _dot_nt(h, w_ref[W_NA:W_MERGE, :])

    def head_norm(a, gain):
        ssq = _dot((a * a).astype(BF16), seg_ref[...])
        return a * lax.rsqrt(ssq * (1.0 / NA_HD) + EPS) * gain

    nq_ref[...] = (head_norm(u[:, :NA_W], gqn_ref[...]) * (NA_HD ** -0.5)).astype(BF16)
    nk = head_norm(u[:, NA_W:2 * NA_W], gkn_ref[...])
    nv = u[:, 2 * NA_W:]
    if transpose_kv:
        seq = nk_ref.shape[2]
        for s in range(nk_ref.shape[0]):
            nk_ref[s] = nk[s * seq:(s + 1) * seq].T
            nv_ref[s] = nv[s * seq:(s + 1) * seq].T
    else:
        nk_ref[...] = nk.astype(nk_ref.dtype)
        nv_ref[...] = nv.astype(nv_ref.dtype)

    sg_ref[...] = jax.nn.sigmoid(_dot_nt(h, w_ref[W_MERGE:W_END, :])).astype(BF16)


def _inproj(x, mod, mod_row, g_mix, w_t, w_lr, b_lr, g_qn, g_kn, seg, batch, seq, transpose_kv):
    n = x.shape[0]
    tm = INPROJ_TILE
    assert (seq % tm == 0 or tm % seq == 0) and W_LR % 16 == 0 and W_NA % 16 == 0 and W_MERGE % 16 == 0
    row = lambda c: pl.BlockSpec((tm, c), lambda i: (i, 0))
    outs = [(GLA_QK, BF16), (GLA_QK, BF16), (GLA_V, BF16), (GLA_V, BF16), (2 * GLA_QK, F32), (NA_W, BF16)]
    out_specs = [row(c) for c, _ in outs]
    out_shape = [jax.ShapeDtypeStruct((n, c), dt) for c, dt in outs]
    if transpose_kv:
        if tm >= seq:
            kv_spec = pl.BlockSpec((tm // seq, NA_W, seq), lambda i: (i, 0, 0))
        else:
            tps = seq // tm
            kv_spec = pl.BlockSpec((1, NA_W, tm), lambda i: (i // tps, 0, i % tps))
        kv_shape = jax.ShapeDtypeStruct((batch, NA_W, seq), F32)
    else:
        kv_spec = row(NA_W)
        kv_shape = jax.ShapeDtypeStruct((n, NA_W), BF16)
    out_specs += [kv_spec, kv_spec, row(2 * D_MODEL)]
    out_shape += [kv_shape, kv_shape, jax.ShapeDtypeStruct((n, 2 * D_MODEL), BF16)]
    return pl.pallas_call(
        functools.partial(_inproj_kernel, transpose_kv),
        grid=(n // tm,),
        in_specs=[row(D_MODEL),
                  pl.BlockSpec((None, 6, 1, D_MODEL), lambda i: (mod_row(i * tm), 0, 0, 0)),
                  _const_spec(g_mix.shape), _const_spec(w_t.shape), _const_spec(w_lr.shape),
                  _const_spec(b_lr.shape), _const_spec(g_qn.shape), _const_spec(g_kn.shape),
                  _const_spec(seg.shape)],
        out_specs=out_specs,
        out_shape=out_shape,
        compiler_params=_params("parallel"),
        name="inproj",
    )(x, mod, g_mix, w_t, w_lr, b_lr, g_qn, g_kn, seg)


GLA_SAFE_DECAY = 80.0


def _chunk_cumsum(x, reverse):
    c = x.shape[0]
    rows = lax.broadcasted_iota(jnp.int32, (c, 1), 0)
    s = 1
    while s < c:
        if reverse:
            x = x + jnp.where(rows < c - s, pltpu.roll(x, c - s, axis=0), 0.0)
        else:
            x = x + jnp.where(rows >= s, pltpu.roll(x, s, axis=0), 0.0)
        s *= 2
    return x


def _gla_kernel(has_s0, nsteps, cps, *refs):
    qf, kf, vf, laf, qb, kb, vb, lab = refs[:8]
    if has_s0:
        s0f, s0b, of, ob, sf, sb, st, oi, q32, bcs = refs[8:]
    else:
        of, ob, sf, sb, st, oi, q32, bcs = refs[8:]
        s0f = s0b = None
    n = pl.program_id(1)
    bt = qf.shape[0]
    C, H, DK, DV = GLA_CHUNK, GLA_HEADS, GLA_DK, GLA_DV
    assert C == DK
    HC = H * C
    row = lax.broadcasted_iota(jnp.int32, (HC, HC), 0)
    col = lax.broadcasted_iota(jnp.int32, (HC, HC), 1)
    same_head = (row // C) == (col // C)
    t_in, s_in = row % C, col % C
    vis = (same_head & (t_in >= s_in), same_head & (t_in <= s_in))
    ins = ((qf, kf, vf, laf, of), (qb, kb, vb, lab, ob))
    order = (list(range(cps)), list(range(cps))[::-1])
    units = [(d, b, c) for c in range(cps) for d in range(2) for b in range(bt)]
    heads = [(slice(h * DK, (h + 1) * DK), slice(h * DV, (h + 1) * DV)) for h in range(H)]

    def chunk(d, which, b, c):
        return ins[d][which][b, c * C:(c + 1) * C, :]

    def rows4(a):
        return jnp.concatenate([a] * H, axis=0)

    def block_diag(a):
        return jnp.where(same_head, rows4(a), jnp.zeros((), a.dtype))

    def v_stack(v):
        return jnp.concatenate([v[:, vs] for _, vs in heads], axis=0)

    @pl.when(n == 0)
    def _init():
        for d, s0 in enumerate((s0f, s0b)):
            for b in range(bt):
                for h in range(H):
                    st[d, b, h * DK:(h + 1) * DK, :] = s0[b, h] if has_s0 else jnp.zeros((DK, DV), F32)

    bc, tot, q_bd, kh_bd, et = {}, {}, {}, {}, {}
    for u in units:
        d, b, c = u
        bc[u] = _chunk_cumsum(chunk(d, 3, b, c), reverse=d == 1)
        tot[u] = bc[u][C - 1:C, :] if d == 0 else bc[u][0:1, :]
    worst = functools.reduce(jnp.maximum, [jnp.max(-tot[u]) for u in units])
    safe = worst <= GLA_SAFE_DECAY

    for u in units:
        d, b, c = u
        q = chunk(d, 0, b, c).astype(F32)
        k = chunk(d, 1, b, c).astype(F32)
        q_bd[u] = block_diag((q * jnp.exp(bc[u])).astype(BF16))
        kh_bd[u] = block_diag((k * jnp.exp(tot[u] - bc[u])).astype(BF16))
        et[u] = jnp.exp(tot[u])

    @pl.when(safe)
    def _intra_factorised():
        for u in units:
            d, b, c = u
            k = chunk(d, 1, b, c).astype(F32)
            kt = rows4((k * jnp.exp(-bc[u])).astype(BF16))
            s = jnp.where(vis[d], _dot_nt(q_bd[u], kt), 0.0).astype(BF16)
            oi[d, b, c] = _dot(s, v_stack(chunk(d, 2, b, c)))

    @pl.when(jnp.logical_not(safe))
    def _intra_pairwise():
        for u in units:
            d, b, c = u
            q32[d, b, c] = chunk(d, 0, b, c).astype(F32)
            bcs[d, b, c] = bc[u]
        svis = lax.broadcasted_iota(jnp.int32, (C, 1), 0)
        for u in units:
            d, b, c = u
            k = chunk(d, 1, b, c).astype(F32)
            v = chunk(d, 2, b, c).astype(F32)
            for h, (ks, vs) in enumerate(heads):
                kh_, bh, vh = k[:, ks], bc[u][:, ks], v[:, vs]

                def one_row(t, carry, d=d, b=b, c=c, h=h, ks=ks, kh_=kh_, bh=bh, vh=vh):
                    bt_row = bcs[d, b, c, pl.ds(t, 1), :][:, ks]
                    q_row = q32[d, b, c, pl.ds(t, 1), :][:, ks]
                    seen = (svis <= t) if d == 0 else (svis >= t)
                    w = jnp.exp(jnp.where(seen, bt_row - bh, -jnp.inf)) * kh_ * q_row
                    s_col = jnp.sum(w, axis=1, keepdims=True)
                    oi[d, b, c, pl.ds(h * C + t, 1), :] = jnp.sum(s_col * vh, axis=0, keepdims=True)
                    return carry

                lax.fori_loop(0, C, one_row, 0)

    ones = jnp.ones((16, DV), BF16)
    for j in range(cps):
        for d in range(2):
            c = order[d][j]
            for b in range(bt):
                u = (d, b, c)
                o = oi[d, b, c] + _dot(q_bd[u], st[d, b].astype(BF16))
                ins[d][4][b, c * C:(c + 1) * C, :] = jnp.concatenate(
                    [o[h * C:(h + 1) * C] for h in range(H)], axis=1).astype(ins[d][4].dtype)
        for d in range(2):
            c = order[d][j]
            for b in range(bt):
                u = (d, b, c)
                e_rows = jnp.concatenate(list(_split3(et[u])) + [jnp.zeros((13, H * DK), BF16)], axis=0)
                decay = _dot_tn(e_rows, ones)
                st[d, b] = decay * st[d, b] + _dot_tn(kh_bd[u], v_stack(chunk(d, 2, b, c)))

    @pl.when(n == nsteps - 1)
    def _fin():
        for b in range(bt):
            for h in range(H):
                sf[b, h] = st[0, b, h * DK:(h + 1) * DK, :]
                sb[b, h] = st[1, b, h * DK:(h + 1) * DK, :]


def _gla(q, k, v, la, s0, batch, seq, bt, cps):
    C = GLA_CHUNK
    rows = cps * C
    assert seq % rows == 0 and batch % bt == 0
    ns = seq // rows
    q, k, v, la = (a.reshape(batch, seq, a.shape[-1]) for a in (q, k, v, la))
    fwd = lambda c, j=0: pl.BlockSpec((bt, rows, c), lambda i, n: (i, n, j))
    bwd = lambda c, j=0: pl.BlockSpec((bt, rows, c), lambda i, n: (i, ns - 1 - n, j))
    st_spec = pl.BlockSpec((bt, GLA_HEADS, GLA_DK, GLA_DV), lambda i, n: (i, 0, 0, 0))
    in_specs = [fwd(GLA_QK), fwd(GLA_QK), fwd(GLA_V), fwd(GLA_QK, 0),
                bwd(GLA_QK), bwd(GLA_QK), bwd(GLA_V), bwd(GLA_QK, 1)]
    args = [q, k, v, la, q, k, v, la]
    if s0 is not None:
        in_specs += [st_spec, st_spec]
        args += list(s0)
    st_shape = jax.ShapeDtypeStruct((batch, GLA_HEADS, GLA_DK, GLA_DV), F32)
    o_shape = jax.ShapeDtypeStruct((batch, seq, GLA_V), BF16)
    of, ob, sf, sb = pl.pallas_call(
        functools.partial(_gla_kernel, s0 is not None, ns, cps),
        grid=(batch // bt, ns),
        in_specs=in_specs,
        out_specs=[fwd(GLA_V), bwd(GLA_V), st_spec, st_spec],
        out_shape=[o_shape, o_shape, st_shape, st_shape],
        scratch_shapes=[pltpu.VMEM((2, bt, GLA_QK, GLA_DV), F32),
                        pltpu.VMEM((2, bt, cps, GLA_HEADS * C, GLA_DV), F32),
                        pltpu.VMEM((2, bt, cps, C, GLA_QK), F32),
                        pltpu.VMEM((2, bt, cps, C, GLA_QK), F32)],
        compiler_params=_params("parallel", "arbitrary"),
        name="gla",
    )(*args)
    return of.reshape(batch * seq, GLA_V), ob.reshape(batch * seq, GLA_V), sf, sb


def _softmax_pv(scores, values):
    m = functools.reduce(jnp.maximum, [jnp.max(s, axis=-1, keepdims=True) for s in scores])
    ps = [jnp.exp(s - m) for s in scores]
    l = functools.reduce(jnp.add, [jnp.sum(p, axis=-1, keepdims=True) for p in ps])
    o = functools.reduce(jnp.add, [(_dot_nt if transposed else _dot)(p.astype(BF16), v)
                                   for p, (v, transposed) in zip(ps, values)])
    return o / l


def _ctx_attn_kernel(q_ref, kt_ref, vt_ref, o_ref):
    q = q_ref[...]
    kt = kt_ref[...].astype(BF16)
    vt = vt_ref[...].astype(BF16)
    outs = []
    for h in range(NA_HEADS):
        hs = slice(h * NA_HD, (h + 1) * NA_HD)
        outs.append(_softmax_pv([_dot(q[:, hs], kt[hs, :])], [(vt[hs, :], True)]))
    o_ref[...] = jnp.concatenate(outs, axis=-1).astype(o_ref.dtype)


def _ctx_attn(q, kt, vt, batch, seq):
    spec = pl.BlockSpec((seq, NA_W), lambda b: (b, 0))
    spec_t = pl.BlockSpec((None, NA_W, seq), lambda b: (b, 0, 0))
    return pl.pallas_call(
        _ctx_attn_kernel,
        grid=(batch,),
        in_specs=[spec, spec_t, spec_t],
        out_specs=spec,
        out_shape=jax.ShapeDtypeStruct((batch * seq, NA_W), BF16),
        compiler_params=_params("parallel"),
        name="ctx_attn",
    )(q, kt, vt)


N_DR = 2 * NA_WIN_ROWS - 1
N_DC = 2 * NA_WIN_COLS - 1
NA_QROWS = 4
NA_KROWS = NA_WIN_ROWS + NA_QROWS
BIAS_BOTH = 0
BIAS_RIGHT = N_DR - 1
BIAS_LEFT = BIAS_RIGHT + NA_WIN_ROWS
BIAS_NONE = BIAS_LEFT + NA_WIN_ROWS
BIAS_ROWS = 32


def _bias_rows(rpb):
    assert 2 * GRID_W == LANES and BIAS_NONE < BIAS_ROWS
    b = jnp.pad(rpb, ((0, 0), (0, 0), (0, GRID_W - N_DC)))
    m = jnp.full((NA_HEADS, NA_WIN_ROWS, GRID_W), MASK_VALUE, F32)
    both = jnp.concatenate([b[:, :-1], b[:, 1:]], axis=-1)
    right = jnp.concatenate([m, b[:, :NA_WIN_ROWS]], axis=-1)
    left = jnp.concatenate([b[:, NA_WIN_ROWS - 1:], m], axis=-1)
    none = jnp.concatenate([m[:, :1], m[:, :1]], axis=-1)
    pad = jnp.zeros((NA_HEADS, BIAS_ROWS - BIAS_NONE - 1, LANES), F32)
    return jnp.concatenate([both, right, left, none, pad], axis=1).reshape(NA_HEADS * BIAS_ROWS, LANES)


def _na_kernel(rows, q_ref, k_ref, v_ref, kct_ref, vct_ref, bias_ref, o_ref, ctx_ref):
    g = pl.program_id(1)

    @pl.when(g == 0)
    def _cast_ctx():
        ctx_ref[0] = kct_ref[...].astype(BF16)
        ctx_ref[1] = vct_ref[...].astype(BF16)

    wr = NA_WIN_ROWS
    ws = jnp.clip(g * NA_QROWS - wr // 2, 0, rows - NA_KROWS)
    start = pl.multiple_of(ws * GRID_W, GRID_W)
    q = q_ref[...]
    kl = k_ref[pl.ds(start, NA_KROWS * GRID_W), :]
    vl = v_ref[pl.ds(start, NA_KROWS * GRID_W), :]

    tile_row = []
    for i in range(NA_QROWS):
        r = g * NA_QROWS + i
        r0 = jnp.clip(r - wr // 2, 0, rows - wr)
        for j in range(NA_KROWS // 2):
            kr = ws + 2 * j
            in0 = (kr >= r0) & (kr < r0 + wr)
            in1 = (kr + 1 >= r0) & (kr + 1 < r0 + wr)
            dr = kr - r + wr - 1
            tile_row.append(jnp.where(in0 & in1, BIAS_BOTH + dr,
                                      jnp.where(in1, BIAS_RIGHT + dr + 1,
                                                jnp.where(in0, BIAS_LEFT + dr - (wr - 1), BIAS_NONE))))

    qc = lax.broadcasted_iota(jnp.int32, (GRID_W, LANES), 0)
    kc = lax.broadcasted_iota(jnp.int32, (GRID_W, LANES), 1) % GRID_W
    c0 = jnp.clip(qc - NA_WIN_COLS // 2, 0, GRID_W - NA_WIN_COLS)
    in_win = (kc >= c0) & (kc < c0 + NA_WIN_COLS)
    shift = LANES - (NA_WIN_COLS - 1)

    def bias_tile(h, t):
        row = jnp.broadcast_to(bias_ref[pl.ds(h * BIAS_ROWS + t, 1), :], (GRID_W, LANES))
        return jnp.where(in_win, pltpu.roll(row, shift, axis=1, stride=1, stride_axis=0), MASK_VALUE)

    outs = []
    for h in range(NA_HEADS):
        hs = slice(h * NA_HD, (h + 1) * NA_HD)
        bias = jnp.concatenate(
            [jnp.concatenate([bias_tile(h, tile_row[i * (NA_KROWS // 2) + j]) for j in range(NA_KROWS // 2)], axis=1)
             for i in range(NA_QROWS)], axis=0)
        s_loc = _dot_nt(q[:, hs], kl[:, hs]) + bias
        s_ctx = _dot(q[:, hs], ctx_ref[0, hs, :])
        outs.append(_softmax_pv([s_loc, s_ctx], [(vl[:, hs], False), (ctx_ref[1, hs, :], True)]))
    o_ref[...] = jnp.concatenate(outs, axis=-1).astype(o_ref.dtype)


def _na_attn(q, k, v, kt_ctx, vt_ctx, bias_rows, batch, seq):
    rows = seq // GRID_W
    wr = NA_WIN_ROWS
    assert rows >= NA_KROWS and rows % NA_QROWS == 0 and wr % 2 == 0 and NA_KROWS % 2 == 0
    for g in range(rows // NA_QROWS):
        ws = min(max(g * NA_QROWS - wr // 2, 0), rows - NA_KROWS)
        for r in range(g * NA_QROWS, (g + 1) * NA_QROWS):
            r0 = min(max(r - wr // 2, 0), rows - wr)
            assert ws <= r0 and r0 + wr <= ws + NA_KROWS
    past = kt_ctx.shape[2]
    q, k, v = (a.reshape(batch, seq, NA_W) for a in (q, k, v))
    whole = pl.BlockSpec((None, seq, NA_W), lambda b, g: (b, 0, 0))
    whole_t = pl.BlockSpec((None, NA_W, past), lambda b, g: (b, 0, 0))
    tile = pl.BlockSpec((None, NA_QROWS * GRID_W, NA_W), lambda b, g: (b, g, 0))
    o = pl.pallas_call(
        functools.partial(_na_kernel, rows),
        grid=(batch, rows // NA_QROWS),
        in_specs=[tile, whole, whole, whole_t, whole_t, _const_spec(bias_rows.shape)],
        out_specs=tile,
        out_shape=jax.ShapeDtypeStruct((batch, seq, NA_W), BF16),
        scratch_shapes=[pltpu.VMEM((2, NA_W, past), BF16)],
        compiler_params=_params("parallel", "arbitrary"),
        name="na_attn",
    )(q, k, v, kt_ctx, vt_ctx, bias_rows)
    return o.reshape(batch * seq, NA_W)


def _merge_body(x, mod_ref, of_ref, ob_ref, gr_ref, na_ref, sg_ref, ggla_ref, wg_ref, wn_ref, wo_ref, gffn_ref):
    o = of_ref[...].astype(F32) + ob_ref[...].astype(F32)
    on = jnp.concatenate([_rms(o[:, h * GLA_DV:(h + 1) * GLA_DV]) for h in range(GLA_HEADS)], axis=-1)
    on = on * ggla_ref[...] * gr_ref[...].astype(F32)
    b_gla = _dot(on.astype(BF16), wg_ref[...])
    b_na = _dot(na_ref[...], wn_ref[...])
    sg = sg_ref[...].astype(F32)
    mix = _dot((sg[:, :D_MODEL] * b_gla + sg[:, D_MODEL:] * b_na).astype(BF16), wo_ref[...])
    x1 = x + mod_ref[2] * mix
    h2 = (_rms(x1) * gffn_ref[...] * (1.0 + mod_ref[4]) + mod_ref[3]).astype(BF16)
    return x1, h2


def _merge_kernel(x_ref, mod_ref, of_ref, ob_ref, gr_ref, na_ref, sg_ref, ggla_ref, wg_ref, wn_ref, wo_ref,
                  gffn_ref, x1_ref, h2_ref):
    x1_ref[...], h2_ref[...] = _merge_body(x_ref[...], mod_ref, of_ref, ob_ref, gr_ref, na_ref, sg_ref, ggla_ref,
                                           wg_ref, wn_ref, wo_ref, gffn_ref)


def _merge(x, mod, mod_row, of, ob, gr, na, sg, g_gla, w_gla_o, w_na_o, w_out, g_ffn):
    n = x.shape[0]
    tm = ROW_TILE
    row = lambda c: pl.BlockSpec((tm, c), lambda i: (i, 0))
    return pl.pallas_call(
        _merge_kernel,
        grid=(n // tm,),
        in_specs=[row(D_MODEL),
                  pl.BlockSpec((None, 6, 1, D_MODEL), lambda i: (mod_row(i * tm), 0, 0, 0)),
                  row(GLA_V), row(GLA_V), row(GLA_V), row(NA_W), row(2 * D_MODEL),
                  _const_spec(g_gla.shape), _const_spec(w_gla_o.shape), _const_spec(w_na_o.shape),
                  _const_spec(w_out.shape), _const_spec(g_ffn.shape)],
        out_specs=[row(D_MODEL), row(D_MODEL)],
        out_shape=[jax.ShapeDtypeStruct((n, D_MODEL), F32), jax.ShapeDtypeStruct((n, D_MODEL), BF16)],
        compiler_params=_params("parallel"),
        name="merge",
    )(x, mod, of, ob, gr, na, sg, g_gla, w_gla_o, w_na_o, w_out, g_ffn)


def _ffn_body(seq, h, wup_ref, wc_ref, bc_ref, wdn_ref, acc_ref, a_ref, g_ref):
    t = h.shape[0]
    pos = lax.broadcasted_iota(jnp.int32, (t, 1), 0) % seq
    first = pos == 0
    last = pos == seq - 1
    tiles = [(lo, min(lo + FF_TILE, D_FF)) for lo in range(0, D_FF, FF_TILE)]

    def up(j):
        lo, hi = tiles[j]
        a_ref[j % 2, :, :hi - lo] = _dot(h, wup_ref[:, lo:hi])
        g_ref[j % 2, :, :hi - lo] = _dot(h, wup_ref[:, D_FF + lo:D_FF + hi])

    def conv(u, lo, hi):
        prev = jnp.where(first, 0.0, pltpu.roll(u, 1, axis=0))
        nxt = jnp.where(last, 0.0, pltpu.roll(u, t - 1, axis=0))
        return wc_ref[0:1, lo:hi] * prev + wc_ref[1:2, lo:hi] * u + wc_ref[2:3, lo:hi] * nxt + bc_ref[:, lo:hi]

    up(0)
    for j, (lo, hi) in enumerate(tiles):
        if j + 1 < len(tiles):
            up(j + 1)
        a = conv(a_ref[j % 2, :, :hi - lo], lo, hi)
        g = conv(g_ref[j % 2, :, :hi - lo], D_FF + lo, D_FF + hi)
        act = (a * (g * jax.nn.sigmoid(g))).astype(BF16)
        half = D_MODEL // 2
        for c0 in (0, half):
            part = _dot(act, wdn_ref[lo:hi, c0:c0 + half])
            if j == 0:
                acc_ref[:, c0:c0 + half] = part
            else:
                acc_ref[:, c0:c0 + half] += part


def _ffn_kernel(seq, h_ref, x1_ref, mod_ref, wup_ref, wc_ref, bc_ref, wdn_ref, o_ref, a_ref, g_ref):
    _ffn_body(seq, h_ref[...], wup_ref, wc_ref, bc_ref, wdn_ref, o_ref, a_ref, g_ref)
    o_ref[...] = x1_ref[...] + mod_ref[5] * o_ref[...]


def _merge_ffn_kernel(seq, x_ref, mod_ref, of_ref, ob_ref, gr_ref, na_ref, sg_ref, ggla_ref, wg_ref, wn_ref, wo_ref,
                      gffn_ref, wup_ref, wc_ref, bc_ref, wdn_ref, o_ref, x1_ref, a_ref, g_ref):
    x1_ref[...], h2 = _merge_body(x_ref[...], mod_ref, of_ref, ob_ref, gr_ref, na_ref, sg_ref, ggla_ref,
                                  wg_ref, wn_ref, wo_ref, gffn_ref)
    _ffn_body(seq, h2, wup_ref, wc_ref, bc_ref, wdn_ref, o_ref, a_ref, g_ref)
    o_ref[...] = x1_ref[...] + mod_ref[5] * o_ref[...]


def _merge_ffn(x, mod, mod_row, of, ob, gr, na, sg, g_gla, w_gla_o, w_na_o, w_out, g_ffn, w_up, w_conv, b_conv,
               w_down, batch, seq, seqs_per_step):
    rows = seq * seqs_per_step
    row = lambda c: pl.BlockSpec((rows, c), lambda i: (i, 0))
    consts = (g_gla, w_gla_o, w_na_o, w_out, g_ffn, w_up, w_conv, b_conv, w_down)
    return pl.pallas_call(
        functools.partial(_merge_ffn_kernel, seq),
        grid=(batch // seqs_per_step,),
        in_specs=[row(D_MODEL),
                  pl.BlockSpec((None, 6, 1, D_MODEL), lambda i: (mod_row(i * rows), 0, 0, 0)),
                  row(GLA_V), row(GLA_V), row(GLA_V), row(NA_W), row(2 * D_MODEL)]
                 + [_const_spec(c.shape) for c in consts],
        out_specs=row(D_MODEL),
        out_shape=jax.ShapeDtypeStruct((batch * seq, D_MODEL), F32),
        scratch_shapes=[pltpu.VMEM((rows, D_MODEL), F32),
                        pltpu.VMEM((2, rows, FF_TILE), F32), pltpu.VMEM((2, rows, FF_TILE), F32)],
        compiler_params=_params("parallel"),
        name="merge_ffn",
    )(x, mod, of, ob, gr, na, sg, *consts)


def _ffn(h2, x1, mod, mod_row, w_up, w_conv, b_conv, w_down, batch, seq, seqs_per_step):
    rows = seq * seqs_per_step
    blk = lambda: pl.BlockSpec((rows, D_MODEL), lambda b: (b, 0))
    return pl.pallas_call(
        functools.partial(_ffn_kernel, seq),
        grid=(batch // seqs_per_step,),
        in_specs=[blk(), blk(),
                  pl.BlockSpec((None, 6, 1, D_MODEL), lambda b: (mod_row(b * rows), 0, 0, 0)),
                  _const_spec(w_up.shape), _const_spec(w_conv.shape), _const_spec(b_conv.shape),
                  _const_spec(w_down.shape)],
        out_specs=blk(),
        out_shape=jax.ShapeDtypeStruct((batch * seq, D_MODEL), F32),
        scratch_shapes=[pltpu.VMEM((2, rows, FF_TILE), F32), pltpu.VMEM((2, rows, FF_TILE), F32)],
        compiler_params=_params("parallel"),
        name="conv_ffn",
    )(h2, x1, mod, w_up, w_conv, b_conv, w_down)


def _layer(x, mod, mod_row, wts, ctx, gla_tile, ffn_seqs, fuse_merge_ffn):
    batch, seq, _ = x.shape
    x2 = x.reshape(batch * seq, D_MODEL)
    q, k, v, gr, la, nq, nk, nv, sg = _inproj(x2, mod, mod_row, wts["g_mix"], wts["w_t"], wts["w_lr"],
                                              wts["b_lr"], wts["g_qn"], wts["g_kn"], wts["seg"], batch, seq,
                                              transpose_kv=ctx is None)
    s0 = None if ctx is None else (ctx[2], ctx[3])
    of, ob, sf, sb = _gla(q, k, v, la, s0, batch, seq, *gla_tile)
    if ctx is None:
        o_na = _ctx_attn(nq, nk, nv, batch, seq)
    else:
        o_na = _na_attn(nq, nk, nv, ctx[0], ctx[1], ctx[4], batch, seq)
    merge_w = (wts["g_gla"], wts["w_gla_o"], wts["w_na_o"], wts["w_out"], wts["g_ffn"])
    ffn_w = (wts["w_up"], wts["w_conv"], wts["b_conv"], wts["w_down"])
    if fuse_merge_ffn:
        y = _merge_ffn(x2, mod, mod_row, of, ob, gr, o_na, sg, *merge_w, *ffn_w, batch, seq, ffn_seqs)
    else:
        x1, h2 = _merge(x2, mod, mod_row, of, ob, gr, o_na, sg, *merge_w)
        y = _ffn(h2, x1, mod, mod_row, *ffn_w, batch, seq, ffn_seqs)
    return y.reshape(batch, seq, D_MODEL), nk, nv, sf, sb


def _pack_weights(l, w_in, w_alpha_fwd, b_alpha_fwd, w_alpha_bwd, b_alpha_bwd, g_norm_mix, g_gla_norm,
                  g_q_norm, g_k_norm, w_gla_o, w_na_o, w_out, g_norm_ffn, w_up, w_conv, b_conv, w_down):
    w_t = _cast_bf16(jnp.swapaxes(w_in[l], 0, 1), CAST_ROWS)
    w_lr = jnp.zeros((LANES, 2 * GLA_QK), F32)
    w_lr = w_lr.at[:GLA_LOWRANK, :GLA_QK].set(w_alpha_fwd[l])
    w_lr = w_lr.at[GLA_LOWRANK:2 * GLA_LOWRANK, GLA_QK:].set(w_alpha_bwd[l]).astype(BF16)
    seg = jnp.asarray(np.kron(np.eye(NA_HEADS), np.ones((NA_HD, NA_HD))), BF16)
    return dict(
        g_mix=g_norm_mix[l][None], w_t=w_t, w_lr=w_lr,
        b_lr=jnp.concatenate([b_alpha_fwd[l], b_alpha_bwd[l]])[None],
        g_qn=jnp.tile(g_q_norm[l], NA_HEADS)[None], g_kn=jnp.tile(g_k_norm[l], NA_HEADS)[None], seg=seg,
        g_gla=jnp.tile(g_gla_norm[l], GLA_HEADS)[None],
        w_gla_o=w_gla_o[l].astype(BF16), w_na_o=w_na_o[l].astype(BF16), w_out=w_out[l].astype(BF16),
        g_ffn=g_norm_ffn[l][None], w_up=w_up[l].astype(BF16), w_conv=w_conv[l], b_conv=b_conv[l][None],
        w_down=w_down[l].astype(BF16))


def kernel(x_prompt, x_sample, c, cache_na_k, cache_na_v, state_gla_fwd, state_gla_bwd, c_ctx, w_ada, b_ada,
           g_norm_mix, w_in, w_alpha_fwd, b_alpha_fwd, w_alpha_bwd, b_alpha_bwd, g_gla_norm, g_q_norm, g_k_norm,
           rpb, w_gla_o, w_na_o, w_out, g_norm_ffn, w_up, w_conv, b_conv, w_down):
    depth = w_in.shape[0]
    batch, seq, _ = x_prompt.shape
    dec_batch, dec_seq, _ = x_sample.shape
    past = cache_na_k.shape[2]
    cond = jnp.concatenate([c_ctx[None], c], axis=0)
    xp, xs = x_prompt, x_sample
    new_k, new_v, new_sf, new_sb = [], [], [], []
    for l in range(depth):
        wts = _pack_weights(l, w_in, w_alpha_fwd, b_alpha_fwd, w_alpha_bwd, b_alpha_bwd, g_norm_mix, g_gla_norm,
                            g_q_norm, g_k_norm, w_gla_o, w_na_o, w_out, g_norm_ffn, w_up, w_conv, b_conv, w_down)
        mod = _ada(cond, w_ada[l], b_ada[l][None]).reshape(1 + dec_batch, 6, 1, D_MODEL)
        tbl = _bias_rows(rpb[l])
        xp, kc, vc, sf, sb = _layer(xp, mod, lambda tok: 0, wts, None, gla_tile=(4, 4), ffn_seqs=2,
                                    fuse_merge_ffn=True)
        to_cache = lambda a: a.reshape(batch, NA_HEADS, NA_HD, seq).transpose(0, 3, 1, 2)
        from_cache = lambda a: a.transpose(0, 2, 3, 1).reshape(dec_batch, NA_W, past)
        new_k.append(to_cache(kc))
        new_v.append(to_cache(vc))
        new_sf.append(sf)
        new_sb.append(sb)
        ctx = (from_cache(cache_na_k[:, l]), from_cache(cache_na_v[:, l]),
               state_gla_fwd[:, l], state_gla_bwd[:, l], tbl)
        xs, _, _, _, _ = _layer(xs, mod, lambda tok: 1 + tok // dec_seq, wts, ctx, gla_tile=(dec_batch, 4),
                               ffn_seqs=1, fuse_merge_ffn=False)
    return (xp, xs, jnp.stack(new_k, axis=1), jnp.stack(new_v, axis=1),
            jnp.stack(new_sf, axis=1), jnp.stack(new_sb, axis=1))
```
